```python
import math
import jax, jax.numpy as jnp
from jax import lax
import numpy as np

D_MODEL = 1024
BATCH = 8
SEQ = 4096
DEPTH = 2

PLE_DIM = 256
ROPE_THETA = 10000.0
NORM_EPS = 1e-6
Q_BLOCK = 128

RET_WIDTH = D_MODEL // 2
RET_HEADS = 8
RET_HEAD_DIM = RET_WIDTH // RET_HEADS
RET_CHUNK = 128
S5_WIDTH = D_MODEL - RET_WIDTH
S5_GROUP = 16
S5_GROUPS = S5_WIDTH // S5_GROUP
S5_STATE = 64
EVEN_IN_WIDTH = 4 * RET_WIDTH + S5_WIDTH
MIX_WIDTH = RET_WIDTH + S5_WIDTH

DIFF_HEADS = 8
DIFF_QK_DIM = 64
DIFF_V_DIM = 2 * DIFF_QK_DIM
DIFF_QK_WIDTH = DIFF_HEADS * 2 * DIFF_QK_DIM
DIFF_WIDTH = DIFF_HEADS * DIFF_V_DIM
DIFF_IN_WIDTH = 2 * DIFF_QK_WIDTH + DIFF_WIDTH

FFN_HIDDEN = -(-8 * D_MODEL // (3 * 256)) * 256

N_EVEN = (DEPTH + 1) // 2
N_ODD = DEPTH // 2

kernel_name = "hybrid_retention_s5_diffattn_block"


def rms_norm(x, gain):
    xf = x.astype(jnp.float32)
    y = xf * lax.rsqrt(jnp.mean(xf * xf, axis=-1, keepdims=True) + NORM_EPS)
    return (y * gain.astype(jnp.float32)).astype(x.dtype)


def head_layer_norm(x):
    mu = jnp.mean(x, axis=-1, keepdims=True)
    xc = x - mu
    return xc * lax.rsqrt(jnp.mean(xc * xc, axis=-1, keepdims=True) + NORM_EPS)


def rotary(x, pos):
    d = x.shape[-1]
    inv = ROPE_THETA ** (-jnp.arange(0, d, 2, dtype=jnp.float32) / d)
    ang = pos.astype(jnp.float32)[:, None] * inv[None, :]
    cos = jnp.concatenate([jnp.cos(ang), jnp.cos(ang)], axis=-1)
    sin = jnp.concatenate([jnp.sin(ang), jnp.sin(ang)], axis=-1)
    xf = x.astype(jnp.float32)
    x1, x2 = xf[..., : d // 2], xf[..., d // 2:]
    rot = jnp.concatenate([-x2, x1], axis=-1)
    return (xf * cos + rot * sin).astype(x.dtype)


def retention(q, k, v):
    bsz, n_h, s_len, d = q.shape
    c = RET_CHUNK
    n_chunks = s_len // c
    gamma = 1.0 - 2.0 ** (-5.0 - jnp.arange(n_h, dtype=jnp.float32))
    log_g = jnp.log(gamma)
    idx = jnp.arange(c, dtype=jnp.float32)
    rel = idx[:, None] - idx[None, :]
    intra_decay = jnp.where(rel >= 0, jnp.exp(log_g[:, None, None] * jnp.maximum(rel, 0.0)), 0.0)
    k_decay = jnp.exp(log_g[:, None] * (c - 1 - idx))
    q_decay = jnp.exp(log_g[:, None] * (idx + 1.0))
    chunk_decay = jnp.exp(log_g * c)
    qf = q.astype(jnp.float32).reshape(bsz, n_h, n_chunks, c, d)
    kf = k.astype(jnp.float32).reshape(bsz, n_h, n_chunks, c, d) * (d ** -0.5)
    vf = v.astype(jnp.float32).reshape(bsz, n_h, n_chunks, c, d)
    scores = jnp.einsum('bhncd,bhnsd->bhncs', qf, kf) * intra_decay[:, None]
    intra = jnp.einsum('bhncs,bhnse->bhnce', scores, vf)
    kv = jnp.einsum('bhnsd,bhnse->nbhde', kf * k_decay[:, None, :, None], vf)

    def step(r_state, kv_n):
        return chunk_decay[:, None, None] * r_state + kv_n, r_state

    _, r_prev = lax.scan(step, jnp.zeros_like(kv[0]), kv)
    inter = jnp.einsum('bhncd,nbhde->bhnce', qf * q_decay[:, None, :, None], r_prev)
    return (intra + inter).reshape(bsz, n_h, s_len, d)


def s5_mixer(u, lam_re, lam_im, b_re, b_im, c_re, c_im, d_skip, log_step, w_glu):
    bsz, s_len, _ = u.shape
    f32 = jnp.float32
    uf = u.astype(f32).reshape(bsz, s_len, S5_GROUPS, S5_GROUP)
    lam = lax.complex(lam_re.astype(f32), lam_im.astype(f32))
    delta = jnp.exp(log_step.astype(f32))[:, None]
    lam_bar = jnp.exp(lam * delta)
    b_mat = lax.complex(b_re.astype(f32), b_im.astype(f32))
    b_bar = ((lam_bar - 1.0) / lam)[:, :, None] * b_mat
    bu = jnp.einsum('gpc,bsgc->bsgp', b_bar, uf.astype(jnp.complex64))
    a = jnp.broadcast_to(lam_bar, bu.shape)

    def combine(e1, e2):
        a1, x1 = e1
        a2, x2 = e2
        return a2 * a1, a2 * x1 + x2

    _, states = lax.associative_scan(combine, (a, bu), axis=1)
    c_mat = lax.complex(c_re.astype(f32), c_im.astype(f32))
    y = jnp.einsum('gcp,bsgp->bsgc', c_mat, states).real + d_skip.astype(f32) * uf
    y = jax.nn.gelu(y.reshape(bsz, s_len, S5_WIDTH))
    y = y * jax.nn.sigmoid(y @ w_glu.astype(f32))
    return y.astype(u.dtype)


def even_mixer(h, pos, w_in, w_out, lam_re, lam_im, b_re, b_im, c_re, c_im, d_skip, log_step, w_glu):
    bsz, s_len, _ = h.shape
    proj = h @ w_in
    q, k, v, g, u = jnp.split(proj, [RET_WIDTH, 2 * RET_WIDTH, 3 * RET_WIDTH, 4 * RET_WIDTH], axis=-1)

    def heads(t):
        return t.reshape(bsz, s_len, RET_HEADS, RET_HEAD_DIM).transpose(0, 2, 1, 3)

    ret = retention(rotary(heads(q), pos), rotary(heads(k), pos), heads(v))
    ret = head_layer_norm(ret).transpose(0, 2, 1, 3).reshape(bsz, s_len, RET_WIDTH)
    ret = (jax.nn.silu(g.astype(jnp.float32)) * ret).astype(h.dtype)
    ssm = s5_mixer(u, lam_re, lam_im, b_re, b_im, c_re, c_im, d_skip, log_step, w_glu)
    return jnp.concatenate([ret, ssm], axis=-1) @ w_out


def diff_attention(h, pos, w_qkv, w_o, lq1, lk1, lq2, lk2, subln, lambda_init):
    bsz, s_len, _ = h.shape
    f32 = jnp.float32
    proj = h @ w_qkv
    q, k, v = jnp.split(proj, [DIFF_QK_WIDTH, 2 * DIFF_QK_WIDTH], axis=-1)
    q = rotary(q.reshape(bsz, s_len, 2 * DIFF_HEADS, DIFF_QK_DIM).transpose(0, 2, 1, 3), pos)
    k = rotary(k.reshape(bsz, s_len, 2 * DIFF_HEADS, DIFF_QK_DIM).transpose(0, 2, 1, 3), pos)
    vf = v.reshape(bsz, s_len, DIFF_HEADS, DIFF_V_DIM).transpose(0, 2, 1, 3).astype(f32)
    kf = k.astype(f32)
    lam = (jnp.exp(jnp.sum(lq1.astype(f32) * lk1.astype(f32)))
           - jnp.exp(jnp.sum(lq2.astype(f32) * lk2.astype(f32))) + lambda_init)
    scale = DIFF_QK_DIM ** -0.5
    n_blocks = s_len // Q_BLOCK
    q_blocks = q.reshape(bsz, 2 * DIFF_HEADS, n_blocks, Q_BLOCK, DIFF_QK_DIM).transpose(2, 0, 1, 3, 4)
    qpos_blocks = pos.reshape(n_blocks, Q_BLOCK)

    def block(args):
        qblk, qpos = args
        s = jnp.einsum('bhqd,bhkd->bhqk', qblk.astype(f32), kf) * scale
        s = jnp.where(pos[None, :] <= qpos[:, None], s, -jnp.inf)
        att = jax.nn.softmax(s, axis=-1).reshape(bsz, DIFF_HEADS, 2, Q_BLOCK, s_len)
        w = att[:, :, 0] - lam * att[:, :, 1]
        return jnp.einsum('bhqk,bhkd->bhqd', w, vf)

    out = lax.map(block, (q_blocks, qpos_blocks))
    out = out.transpose(1, 2, 0, 3, 4).reshape(bsz, DIFF_HEADS, s_len, DIFF_V_DIM)
    out = rms_norm(out, subln) * (1.0 - lambda_init)
    out = out.transpose(0, 2, 1, 3).reshape(bsz, s_len, DIFF_WIDTH).astype(h.dtype)
    return out @ w_o


def swiglu(h, w_gate, w_up, w_down):
    return (jax.nn.silu(h @ w_gate) * (h @ w_up)) @ w_down


def setup_inputs(seed: int = 0) -> dict:
    key = jax.random.key(seed)
    ks = jax.random.split(key, 32)
    nrm = jax.random.normal
    f32 = jnp.float32
    inp = {}
    inp['x'] = nrm(ks[0], (BATCH, SEQ, D_MODEL), f32)
    inp['p'] = nrm(ks[1], (DEPTH, BATCH, SEQ, PLE_DIM), f32)
    inp['norm_mix'] = 1.0 + 0.02 * nrm(ks[2], (DEPTH, D_MODEL), f32)
    inp['norm_ffn'] = 1.0 + 0.02 * nrm(ks[3], (DEPTH, D_MODEL), f32)
    inp['norm_ple'] = 1.0 + 0.02 * nrm(ks[4], (DEPTH, D_MODEL), f32)
    inp['ret_s5_w_in'] = nrm(ks[5], (N_EVEN, D_MODEL, EVEN_IN_WIDTH), f32) * D_MODEL ** -0.5
    inp['ret_s5_w_out'] = nrm(ks[6], (N_EVEN, MIX_WIDTH, D_MODEL), f32) * MIX_WIDTH ** -0.5
    inp['s5_lambda_re'] = -0.5 + 0.01 * nrm(ks[7], (N_EVEN, S5_GROUPS, S5_STATE), f32)
    inp['s5_lambda_im'] = (math.pi * jnp.arange(S5_STATE, dtype=f32)
                           + 0.01 * nrm(ks[8], (N_EVEN, S5_GROUPS, S5_STATE), f32))
    inp['s5_b_re'] = nrm(ks[9], (N_EVEN, S5_GROUPS, S5_STATE, S5_GROUP), f32) * (2 * S5_GROUP) ** -0.5
    inp['s5_b_im'] = nrm(ks[10], (N_EVEN, S5_GROUPS, S5_STATE, S5_GROUP), f32) * (2 * S5_GROUP) ** -0.5
    inp['s5_c_re'] = nrm(ks[11], (N_EVEN, S5_GROUPS, S5_GROUP, S5_STATE), f32) * S5_STATE ** -0.5
    inp['s5_c_im'] = nrm(ks[12], (N_EVEN, S5_GROUPS, S5_GROUP, S5_STATE), f32) * S5_STATE ** -0.5
    inp['s5_d'] = nrm(ks[13], (N_EVEN, S5_GROUPS, S5_GROUP), f32)
    inp['s5_log_step'] = jax.random.uniform(ks[14], (N_EVEN, S5_GROUPS), f32,
                                            minval=math.log(1e-3), maxval=math.log(1e-1))
    inp['s5_w_glu'] = nrm(ks[15], (N_EVEN, S5_WIDTH, S5_WIDTH), f32) * S5_WIDTH ** -0.5
    inp['diff_w_qkv'] = nrm(ks[16], (N_ODD, D_MODEL, DIFF_IN_WIDTH), f32) * D_MODEL ** -0.5
    inp['diff_w_o'] = nrm(ks[17], (N_ODD, DIFF_WIDTH, D_MODEL), f32) * DIFF_WIDTH ** -0.5
    inp['diff_lambda_q1'] = 0.1 * nrm(ks[18], (N_ODD, DIFF_QK_DIM), f32)
    inp['diff_lambda_k1'] = 0.1 * nrm(ks[19], (N_ODD, DIFF_QK_DIM), f32)
    inp['diff_lambda_q2'] = 0.1 * nrm(ks[20], (N_ODD, DIFF_QK_DIM), f32)
    inp['diff_lambda_k2'] = 0.1 * nrm(ks[21], (N_ODD, DIFF_QK_DIM), f32)
    inp['diff_subln'] = 1.0 + 0.02 * nrm(ks[22], (N_ODD, DIFF_V_DIM), f32)
    inp['ffn_w_gate'] = nrm(ks[23], (DEPTH, D_MODEL, FFN_HIDDEN), f32) * D_MODEL ** -0.5
    inp['ffn_w_up'] = nrm(ks[24], (DEPTH, D_MODEL, FFN_HIDDEN), f32) * D_MODEL ** -0.5
    inp['ffn_w_down'] = nrm(ks[25], (DEPTH, FFN_HIDDEN, D_MODEL), f32) * FFN_HIDDEN ** -0.5
    inp['ple_w_proj'] = nrm(ks[26], (DEPTH, PLE_DIM, D_MODEL), f32) * PLE_DIM ** -0.5
    inp['ple_w_gate'] = nrm(ks[27], (DEPTH, D_MODEL, D_MODEL), f32) * D_MODEL ** -0.5
    inp['final_norm'] = 1.0 + 0.02 * nrm(ks[28], (D_MODEL,), f32)
    return inp


def reference(x, p, norm_mix, norm_ffn, norm_ple, ret_s5_w_in, ret_s5_w_out,
              s5_lambda_re, s5_lambda_im, s5_b_re, s5_b_im, s5_c_re, s5_c_im, s5_d,
              s5_log_step, s5_w_glu, diff_w_qkv, diff_w_o, diff_lambda_q1, diff_lambda_k1,
              diff_lambda_q2, diff_lambda_k2, diff_subln, ffn_w_gate, ffn_w_up, ffn_w_down,
              ple_w_proj, ple_w_gate, final_norm):
    pos = jnp.arange(x.shape[1], dtype=jnp.int32)
    h = x
    for i in range(DEPTH):
        hn = rms_norm(h, norm_mix[i])
        j = i // 2
        if i % 2 == 0:
            mix = even_mixer(hn, pos, ret_s5_w_in[j], ret_s5_w_out[j],
                             s5_lambda_re[j], s5_lambda_im[j], s5_b_re[j], s5_b_im[j],
                             s5_c_re[j], s5_c_im[j], s5_d[j], s5_log_step[j], s5_w_glu[j])
        else:
            lambda_init = 0.8 - 0.6 * math.exp(-0.3 * i)
            mix = diff_attention(hn, pos, diff_w_qkv[j], diff_w_o[j],
                                 diff_lambda_q1[j], diff_lambda_k1[j],
                                 diff_lambda_q2[j], diff_lambda_k2[j], diff_subln[j], lambda_init)
        h = h + mix
        h = h + swiglu(rms_norm(h, norm_ffn[i]), ffn_w_gate[i], ffn_w_up[i], ffn_w_down[i])
        gate = jax.nn.sigmoid(rms_norm(h, norm_ple[i]) @ ple_w_gate[i])
        h = h + (p[i] @ ple_w_proj[i]) * gate
    return rms_norm(h, final_norm)
```

```python
import functools
import math

import jax
import jax.numpy as jnp
from jax import lax
from jax.experimental import pallas as pl
from jax.experimental.pallas import tpu as pltpu

F32 = jnp.float32
BF16 = jnp.bfloat16

NORM_EPS = 1e-6
ROPE_THETA = 10000.0
HEAD_DIM = 64
PAIR = 2 * HEAD_DIM
RET_CHUNK = 128
S5_GROUP = 16
S5_STATE = 64
S5_OCTET = 8
LAMBDA_INIT_ODD = 0.8 - 0.6 * math.exp(-0.3 * 1)

V7X_VMEM_LIMIT_BYTES = 56 * 1024 * 1024
ROW_TILE = 512
PROJ_COL_CHUNK = 512
RET_STEP = 512
S5_STEP = 128
S5_PANEL = 512
ATTN_TQ = 512
ATTN_TK = 512


def _params(*semantics):
    return pltpu.CompilerParams(dimension_semantics=semantics,
                                vmem_limit_bytes=V7X_VMEM_LIMIT_BYTES)


def _resident(shape):
    zeros = (0,) * len(shape)
    return pl.BlockSpec(shape, lambda *_: zeros, pipeline_mode=pl.Buffered(1))


def _rms(x, gain):
    return x * lax.rsqrt(jnp.mean(x * x, axis=-1, keepdims=True) + NORM_EPS) * gain


def _sigmoid(x):
    return 1.0 / (1.0 + jnp.exp(-x))


def _dot(a, b):
    return jnp.dot(a, b, preferred_element_type=F32)


def _dot_nt(a, b):
    return lax.dot_general(a, b, (((1,), (1,)), ((), ())), preferred_element_type=F32)


def _dot_tn(a, b):
    return lax.dot_general(a, b, (((0,), (0,)), ((), ())), preferred_element_type=F32)


def _rope_tables(seq, scale):
    inv = ROPE_THETA ** (-jnp.arange(0, HEAD_DIM, 2, dtype=F32) / HEAD_DIM)
    ang = jnp.arange(seq, dtype=F32)[:, None] * inv[None, :]
    reps = PAIR // (HEAD_DIM // 2)
    cos = jnp.tile(jnp.cos(ang), (1, reps))
    sin = jnp.tile(jnp.sin(ang), (1, reps))
    first_half = (jnp.arange(PAIR) % HEAD_DIM) < HEAD_DIM // 2
    sin = jnp.where(first_half[None, :], -sin, sin)
    return cos * scale, sin * scale


def _norm_proj_kernel(x_ref, g_ref, w_ref, cq_ref, sq_ref, ck_ref, sk_ref, *out_refs,
                      n_q, n_k, n_main):
    hn = _rms(x_ref[...], g_ref[...]).astype(BF16)
    chunk = PROJ_COL_CHUNK
    lane = lax.broadcasted_iota(jnp.int32, (1, chunk), 1)
    first_half = (lane % HEAD_DIM) < HEAD_DIM // 2
    reps = chunk // PAIR
    for lo in range(0, w_ref.shape[1], chunk):
        acc = _dot(hn, w_ref[:, lo:lo + chunk])
        if lo < n_q + n_k:
            cos_ref, sin_ref = (cq_ref, sq_ref) if lo < n_q else (ck_ref, sk_ref)
            cos = jnp.concatenate([cos_ref[...]] * reps, axis=1)
            sin = jnp.concatenate([sin_ref[...]] * reps, axis=1)
            rot = jnp.where(first_half,
                            pltpu.roll(acc, chunk - HEAD_DIM // 2, 1),
                            pltpu.roll(acc, HEAD_DIM // 2, 1))
            acc = acc * cos + rot * sin
        if lo < n_main:
            out_refs[0][:, lo:lo + chunk] = acc.astype(BF16)
        else:
            out_refs[1][:, lo - n_main:lo - n_main + chunk] = acc.astype(BF16)


def _norm_proj(x2d, gain, w, batch, seq, *, n_q, n_k, n_main):
    tokens, d_model = x2d.shape
    n_total = w.shape[1]
    n_rest = n_total - n_main
    tm = min(ROW_TILE, seq)
    nt = seq // tm
    q_scale = HEAD_DIM ** -0.5
    cq, sq = _rope_tables(seq, q_scale)
    ck, sk = _rope_tables(seq, 1.0)
    table_spec = pl.BlockSpec((tm, PAIR), lambda b, j: (j, 0))
    out_shape = [jax.ShapeDtypeStruct((tokens, n_main), BF16)]
    out_specs = [pl.BlockSpec((tm, n_main), lambda b, j: (b * nt + j, 0))]
    if n_rest:
        out_shape.append(jax.ShapeDtypeStruct((seq, batch * n_rest), BF16))
        out_specs.append(pl.BlockSpec((tm, n_rest), lambda b, j: (j, b)))
    return pl.pallas_call(
        functools.partial(_norm_proj_kernel, n_q=n_q, n_k=n_k, n_main=n_main),
        grid=(batch, nt),
        in_specs=[
            pl.BlockSpec((tm, d_model), lambda b, j: (b * nt + j, 0)),
            _resident((1, d_model)),
            _resident((d_model, n_total)),
            table_spec, table_spec, table_spec, table_spec,
        ],
        out_specs=out_specs,
        out_shape=out_shape,
        compiler_params=_params("parallel", "parallel"),
        name="norm_proj_rope",
    )(x2d, gain.reshape(1, d_model), w.astype(BF16), cq, sq, ck, sk)


def _retention_tables(n_heads):
    c = RET_CHUNK
    gamma = 1.0 - 2.0 ** (-5.0 - jnp.arange(n_heads, dtype=F32))
    log_g = jnp.log(gamma)
    idx = jnp.arange(c, dtype=F32)
    rel = idx[:, None] - idx[None, :]
    intra = jnp.where(rel >= 0, jnp.exp(log_g[:, None, None] * jnp.maximum(rel, 0.0)), 0.0)
    zeta = jnp.exp(log_g[:, None] * (c - 1 - idx))
    xi = jnp.exp(log_g[:, None] * (idx + 1.0))
    chunk_decay = jnp.exp(log_g * c)
    per_lane = lambda t: jnp.repeat(t.T, HEAD_DIM, axis=1)
    head_of = jnp.arange(PAIR) // HEAD_DIM
    same_head = (head_of[:, None] == head_of[None, :]).astype(F32)
    pair_decay = chunk_decay.reshape(n_heads // 2, 2)[:, head_of]
    state_decay = pair_decay[:, :, None] * same_head[None]
    return intra, per_lane(zeta), per_lane(xi), state_decay, same_head


def _retention_kernel(q_ref, k_ref, v_ref, g_ref, intra_ref, zeta_ref, xi_ref,
                      sdec_ref, same_ref, o_ref, state_ref):
    n_pairs = state_ref.shape[0]
    c = RET_CHUNK

    @pl.when(pl.program_id(1) == 0)
    def _():
        state_ref[...] = jnp.zeros(state_ref.shape, F32)

    lane = lax.broadcasted_iota(jnp.int32, (c, PAIR), 1)
    left = lane < HEAD_DIM
    same = same_ref[...]
    head_mean = (same * (1.0 / HEAD_DIM)).astype(BF16)

    def chunk_body(ci, carry):
        rows = pl.ds(pl.multiple_of(ci * c, c), c)
        for p in range(n_pairs):
            cols = slice(p * PAIR, (p + 1) * PAIR)
            qp = q_ref[0, rows, cols]
            kp = k_ref[0, rows, cols]
            vp = v_ref[0, rows, cols]
            zero = jnp.zeros_like(qp)
            s0 = _dot_nt(jnp.where(left, qp, zero), kp) * intra_ref[2 * p]
            s1 = _dot_nt(jnp.where(left, zero, qp), kp) * intra_ref[2 * p + 1]
            intra = jnp.where(left, _dot(s0.astype(BF16), vp), _dot(s1.astype(BF16), vp))
            state = state_ref[p]
            q_dec = (qp.astype(F32) * xi_ref[:, cols]).astype(BF16)
            out = intra + _dot(q_dec, state.astype(BF16))
            k_dec = (kp.astype(F32) * zeta_ref[:, cols]).astype(BF16)
            state_ref[p] = sdec_ref[p] * state + _dot_tn(k_dec, vp) * same
            mu = _dot(out.astype(BF16), head_mean)
            xc = out - mu
            var = _dot((xc * xc).astype(BF16), head_mean)
            y = xc * lax.rsqrt(var + NORM_EPS)
            g = g_ref[0, rows, cols].astype(F32)
            o_ref[0, rows, cols] = (g * _sigmoid(g) * y).astype(BF16)
        return carry

    lax.fori_loop(0, q_ref.shape[1] // c, chunk_body, 0)


def _retention(proj3d, width):
    batch, seq, _ = proj3d.shape
    n_heads = width // HEAD_DIM
    tc = min(RET_STEP, seq)
    intra, zeta, xi, sdec, same = _retention_tables(n_heads)
    col_spec = lambda i: pl.BlockSpec((1, tc, width), lambda b, j: (b, j, i))
    return pl.pallas_call(
        _retention_kernel,
        grid=(batch, seq // tc),
        in_specs=[col_spec(0), col_spec(1), col_spec(2), col_spec(3),
                  _resident(intra.shape), _resident(zeta.shape), _resident(xi.shape),
                  _resident(sdec.shape), _resident(same.shape)],
        out_specs=pl.BlockSpec((1, tc, width), lambda b, j: (b, j, 0)),
        out_shape=jax.ShapeDtypeStruct((batch, seq, width), BF16),
        scratch_shapes=[pltpu.VMEM((n_heads // 2, PAIR, PAIR), F32)],
        compiler_params=_params("parallel", "arbitrary"),
        name="retention",
    )(proj3d, proj3d, proj3d, proj3d, intra, zeta, xi, sdec, same)


def _s5_tables(lam_re, lam_im, b_re, b_im, c_re, c_im, log_step):
    n_groups = lam_re.shape[0]
    n_oct = n_groups // S5_OCTET
    lam = lax.complex(lam_re.astype(F32), lam_im.astype(F32))
    delta = jnp.exp(log_step.astype(F32))[:, None]
    lam_bar = jnp.exp(lam * delta)
    b_bar = ((lam_bar - 1.0) / lam)[:, :, None] * lax.complex(b_re.astype(F32), b_im.astype(F32))
    eye = jnp.eye(S5_OCTET, dtype=F32)

    def expand(t, order):
        t = t.reshape(2, n_oct, S5_OCTET, S5_STATE, S5_GROUP)
        t = t[:, :, :, :, None, :] * eye[None, None, :, None, :, None]
        return jnp.transpose(t, order)

    n_state = 2 * n_groups * S5_STATE
    b_parts = jnp.stack([jnp.real(b_bar), jnp.imag(b_bar)])
    b_mat = expand(b_parts, (4, 5, 0, 1, 2, 3)).reshape(S5_OCTET * S5_GROUP, n_state)
    c_parts = jnp.stack([c_re.astype(F32), -c_im.astype(F32)])
    c_mat = expand(jnp.swapaxes(c_parts, 2, 3), (0, 1, 2, 3, 4, 5)).reshape(
        n_state, S5_OCTET * S5_GROUP)
    a_re = jnp.real(lam_bar).reshape(1, n_groups * S5_STATE)
    a_im = jnp.imag(lam_bar).reshape(1, n_groups * S5_STATE)
    return a_re, a_im, b_mat.astype(BF16), c_mat.astype(BF16)


def _s5_kernel(u_ref, are_ref, aim_ref, b_ref, c_ref, d_ref, wglu_ref, o_ref,
               xs_ref, carry_ref, *, batch):
    n_half = are_ref.shape[1]
    oct_in = S5_OCTET * S5_GROUP
    oct_st = S5_OCTET * S5_STATE
    n_oct = n_half // oct_st
    steps = u_ref.shape[0] // batch

    @pl.when(pl.program_id(0) == 0)
    def _():
        carry_ref[...] = jnp.zeros(carry_ref.shape, F32)

    for part in range(2):
        for o in range(n_oct):
            cols = slice(part * n_half + o * oct_st, part * n_half + (o + 1) * oct_st)
            xs_ref[:, cols] = _dot(u_ref[:, o * oct_in:(o + 1) * oct_in], b_ref[:, cols])

    for lo in range(0, n_half, S5_PANEL):
        re_cols = slice(lo, lo + S5_PANEL)
        im_cols = slice(n_half + lo, n_half + lo + S5_PANEL)
        a_re = jnp.broadcast_to(are_ref[:, re_cols], (batch, S5_PANEL))
        a_im = jnp.broadcast_to(aim_ref[:, re_cols], (batch, S5_PANEL))

        def step(t, x):
            x_re, x_im = x
            rows = pl.ds(pl.multiple_of(t * batch, batch), batch)
            n_re = (a_re * x_re - a_im * x_im) + xs_ref[rows, re_cols]
            n_im = (a_re * x_im + a_im * x_re) + xs_ref[rows, im_cols]
            xs_ref[rows, re_cols] = n_re
            xs_ref[rows, im_cols] = n_im
            return n_re, n_im

        x_re, x_im = lax.fori_loop(0, steps, step,
                                   (carry_ref[:, re_cols], carry_ref[:, im_cols]), unroll=8)
        carry_ref[:, re_cols] = x_re
        carry_ref[:, im_cols] = x_im

    ys = []
    for o in range(n_oct):
        re_cols = slice(o * oct_st, (o + 1) * oct_st)
        im_cols = slice(n_half + o * oct_st, n_half + (o + 1) * oct_st)
        ys.append(_dot(xs_ref[:, re_cols].astype(BF16), c_ref[re_cols, :])
                  + _dot(xs_ref[:, im_cols].astype(BF16), c_ref[im_cols, :]))
    y = jnp.concatenate(ys, axis=1) + d_ref[...] * u_ref[...].astype(F32)
    y = jax.nn.gelu(y)
    o_ref[...] = (y * _sigmoid(_dot(y.astype(BF16), wglu_ref[...]))).astype(BF16)


def _s5(u_tm, batch, a_re, a_im, b_mat, c_mat, d_skip, w_glu):
    rows, width = u_tm.shape
    seq = rows // batch
    ts = min(S5_STEP, seq)
    n_state = b_mat.shape[1]
    return pl.pallas_call(
        functools.partial(_s5_kernel, batch=batch),
        grid=(seq // ts,),
        in_specs=[pl.BlockSpec((ts * batch, width), lambda j: (j, 0)),
                  _resident(a_re.shape), _resident(a_im.shape),
                  _resident(b_mat.shape), _resident(c_mat.shape),
                  _resident((1, width)), _resident(w_glu.shape)],
        out_specs=pl.BlockSpec((ts * batch, width), lambda j: (j, 0)),
        out_shape=jax.ShapeDtypeStruct((rows, width), BF16),
        scratch_shapes=[pltpu.VMEM((ts * batch, n_state), F32),
                        pltpu.VMEM((batch, n_state), F32)],
        compiler_params=_params("arbitrary"),
        name="s5_scan",
    )(u_tm, a_re, a_im, b_mat, c_mat, d_skip.reshape(1, width).astype(F32), w_glu.astype(BF16))


def _diff_attn_kernel(q_ref, k_ref, v_ref, lam_ref, subln_ref, o_ref,
                      m_ref, l_ref, acc_ref):
    tq = q_ref.shape[1]
    tk = ATTN_TK if k_ref.shape[1] >= ATTN_TK else k_ref.shape[1]
    qi = pl.program_id(2)
    q = q_ref[0]
    lane = lax.broadcasted_iota(jnp.int32, q.shape, 1)
    zero = jnp.zeros_like(q)
    q_maps = (jnp.where(lane < HEAD_DIM, q, zero), jnp.where(lane < HEAD_DIM, zero, q))

    m_ref[...] = jnp.full(m_ref.shape, -jnp.inf, F32)
    l_ref[...] = jnp.zeros(l_ref.shape, F32)
    acc_ref[...] = jnp.zeros(acc_ref.shape, F32)

    def block(kj, masked):
        rows = pl.ds(pl.multiple_of(kj * tk, tk), tk)
        k = k_ref[0, rows, :]
        v = v_ref[0, rows, :]
        for i in range(2):
            s = _dot_nt(q_maps[i], k)
            if masked:
                q_pos = qi * tq + lax.broadcasted_iota(jnp.int32, s.shape, 0)
                k_pos = kj * tk + lax.broadcasted_iota(jnp.int32, s.shape, 1)
                s = jnp.where(k_pos <= q_pos, s, -jnp.inf)
            m_prev = m_ref[i]
            m_new = jnp.maximum(m_prev, jnp.max(s, axis=-1, keepdims=True))
            alpha = jnp.exp(m_prev - m_new)
            p = jnp.exp(s - m_new)
            l_ref[i] = alpha * l_ref[i] + jnp.sum(p, axis=-1, keepdims=True)
            acc_ref[i] = alpha * acc_ref[i] + _dot(p.astype(BF16), v)
            m_ref[i] = m_new

    n_full = (qi * tq) // tk

    def full_body(kj, carry):
        block(kj, masked=False)
        return carry

    lax.fori_loop(0, n_full, full_body, 0)
    for d in range(tq // tk):
        block(n_full + d, masked=True)

    lam_p = lam_ref[...]
    lam = (jnp.exp(jnp.sum(lam_p[0:1] * lam_p[1:2], axis=-1, keepdims=True))
           - jnp.exp(jnp.sum(lam_p[2:3] * lam_p[3:4], axis=-1, keepdims=True))
           + LAMBDA_INIT_ODD)
    out = acc_ref[0] / l_ref[0] - lam * (acc_ref[1] / l_ref[1])
    o_ref[0] = (_rms(out, subln_ref[...]) * (1.0 - LAMBDA_INIT_ODD)).astype(BF16)


def _diff_attention(qkv3d, n_heads, lam_params, subln):
    batch, seq, _ = qkv3d.shape
    tq = min(ATTN_TQ, seq)
    return pl.pallas_call(
        _diff_attn_kernel,
        grid=(batch, n_heads, seq // tq),
        in_specs=[pl.BlockSpec((1, tq, PAIR), lambda b, h, i: (b, i, h)),
                  pl.BlockSpec((1, seq, PAIR), lambda b, h, i: (b, 0, n_heads + h)),
                  pl.BlockSpec((1, seq, PAIR), lambda b, h, i: (b, 0, 2 * n_heads + h)),
                  _resident(lam_params.shape), _resident((1, PAIR))],
        out_specs=pl.BlockSpec((1, tq, PAIR), lambda b, h, i: (b, i, h)),
        out_shape=jax.ShapeDtypeStruct((batch, seq, n_heads * PAIR), BF16),
        scratch_shapes=[pltpu.VMEM((2, tq, 1), F32), pltpu.VMEM((2, tq, 1), F32),
                        pltpu.VMEM((2, tq, PAIR), F32)],
        compiler_params=_params("parallel", "parallel", "arbitrary"),
        name="diff_attention",
    )(qkv3d, qkv3d, qkv3d, lam_params, subln.reshape(1, PAIR).astype(F32))


def _ffn_chunks(hidden):
    bounds = list(range(0, hidden, 1024)) + [hidden]
    return list(zip(bounds[:-1], bounds[1:]))


def _post_kernel(*refs, n_mix, final):
    h_ref = refs[0]
    mix_refs = refs[1:1 + n_mix]
    (p_ref, wo_ref, nf_ref, wg_ref, wu_ref, wd_ref, np_ref, wpg_ref, wpp_ref,
     fin_ref, o_ref) = refs[1 + n_mix:]
    h = h_ref[...]
    row = 0
    for m_ref in mix_refs:
        width = m_ref.shape[1]
        h = h + _dot(m_ref[...], wo_ref[row:row + width, :])
        row += width
    hn = _rms(h, nf_ref[...]).astype(BF16)
    ffn = jnp.zeros_like(h)
    for lo, hi in _ffn_chunks(wg_ref.shape[1]):
        gate = _dot(hn, wg_ref[:, lo:hi])
        up = _dot(hn, wu_ref[:, lo:hi])
        ffn = ffn + _dot((gate * _sigmoid(gate) * up).astype(BF16), wd_ref[lo:hi, :])
    h = h + ffn
    hp = _rms(h, np_ref[...]).astype(BF16)
    ple_gate = _sigmoid(_dot(hp, wpg_ref[...]))
    h = h + _dot(p_ref[...].astype(BF16), wpp_ref[...]) * ple_gate
    if final:
        h = _rms(h, fin_ref[...])
    o_ref[...] = h


def _post_mixer(h2d, mixes, p2d, w_out, norm_ffn, w_gate, w_up, w_down,
                norm_ple, w_ple_gate, w_ple_proj, final_norm, batch, seq, *, final):
    tokens, d_model = h2d.shape
    tm = min(ROW_TILE, seq)
    nt = seq // tm
    row_spec = lambda width: pl.BlockSpec((tm, width), lambda b, j: (b * nt + j, 0))
    vec = lambda v: v.reshape(1, d_model).astype(F32)
    weights = [w_out.astype(BF16), vec(norm_ffn), w_gate.astype(BF16), w_up.astype(BF16),
               w_down.astype(BF16), vec(norm_ple), w_ple_gate.astype(BF16),
               w_ple_proj.astype(BF16), vec(final_norm)]
    mix_arrays = [m for m, _ in mixes]
    mix_specs = [spec(tm, nt) for _, spec in mixes]
    return pl.pallas_call(
        functools.partial(_post_kernel, n_mix=len(mixes), final=final),
        grid=(batch, nt),
        in_specs=([row_spec(d_model)] + mix_specs + [row_spec(p2d.shape[1])]
                  + [_resident(w.shape) for w in weights]),
        out_specs=row_spec(d_model),
        out_shape=jax.ShapeDtypeStruct((tokens, d_model), F32),
        compiler_params=_params("parallel", "parallel"),
        name="outproj_ffn_ple",
    )(h2d, *mix_arrays, p2d, *weights)


def _batch_major(width):
    return lambda tm, nt: pl.BlockSpec((tm, width), lambda b, j: (b * nt + j, 0))


def _time_major(width):
    return lambda tm, nt: pl.BlockSpec((tm, width), lambda b, j: (j, b))


def kernel(x, p, norm_mix, norm_ffn, norm_ple, ret_s5_w_in, ret_s5_w_out, s5_lambda_re, s5_lambda_im, s5_b_re, s5_b_im, s5_c_re, s5_c_im, s5_d, s5_log_step, s5_w_glu, diff_w_qkv, diff_w_o, diff_lambda_q1, diff_lambda_k1, diff_lambda_q2, diff_lambda_k2, diff_subln, ffn_w_gate, ffn_w_up, ffn_w_down, ple_w_proj, ple_w_gate, final_norm):
    batch, seq, d_model = x.shape
    tokens = batch * seq
    h = x.reshape(tokens, d_model)
    p2d = p.reshape(p.shape[0], tokens, p.shape[-1])

    s5_width = s5_w_glu.shape[-1]
    ret_width = ret_s5_w_in.shape[-1] - s5_width
    ret_width //= 4
    proj, u_tm = _norm_proj(h, norm_mix[0], ret_s5_w_in[0], batch, seq,
                            n_q=ret_width, n_k=ret_width, n_main=4 * ret_width)
    ret = _retention(proj.reshape(batch, seq, 4 * ret_width), ret_width)
    a_re, a_im, b_mat, c_mat = _s5_tables(s5_lambda_re[0], s5_lambda_im[0], s5_b_re[0], s5_b_im[0],
                                          s5_c_re[0], s5_c_im[0], s5_log_step[0])
    ssm_tm = _s5(u_tm.reshape(seq * batch, s5_width), batch, a_re, a_im, b_mat, c_mat,
                 s5_d[0], s5_w_glu[0])
    h = _post_mixer(h, [(ret.reshape(tokens, ret_width), _batch_major(ret_width)),
                        (ssm_tm.reshape(seq, batch * s5_width), _time_major(s5_width))],
                    p2d[0], ret_s5_w_out[0], norm_ffn[0], ffn_w_gate[0], ffn_w_up[0],
                    ffn_w_down[0], norm_ple[0], ple_w_gate[0], ple_w_proj[0], final_norm,
                    batch, seq, final=False)

    v_width = diff_w_o.shape[1]
    n_heads = v_width // PAIR
    (qkv,) = _norm_proj(h, norm_mix[1], diff_w_qkv[0], batch, seq,
                        n_q=v_width, n_k=v_width, n_main=3 * v_width)
    lam_params = jnp.stack([diff_lambda_q1[0], diff_lambda_k1[0],
                            diff_lambda_q2[0], diff_lambda_k2[0]]).astype(F32)
    attn = _diff_attention(qkv.reshape(batch, seq, 3 * v_width), n_heads, lam_params, diff_subln[0])
    h = _post_mixer(h, [(attn.reshape(tokens, v_width), _batch_major(v_width))],
                    p2d[1], diff_w_o[0], norm_ffn[1], ffn_w_gate[1], ffn_w_up[1],
                    ffn_w_down[1], norm_ple[1], ple_w_gate[1], ple_w_proj[1], final_norm,
                    batch, seq, final=True)
    return h.reshape(batch, seq, d_model)
```

```python
import functools
import math

import jax
import jax.numpy as jnp
from jax import lax
from jax.experimental import pallas as pl
from jax.experimental.pallas import tpu as pltpu

F32 = jnp.float32
BF16 = jnp.bfloat16

NORM_EPS = 1e-6
ROPE_THETA = 10000.0
HEAD_DIM = 64
PAIR = 2 * HEAD_DIM
RET_CHUNK = 128
S5_GROUP = 16
S5_STATE = 64
S5_OCTET = 8
LAMBDA_INIT_ODD = 0.8 - 0.6 * math.exp(-0.3 * 1)

V7X_VMEM_LIMIT_BYTES = 56 * 1024 * 1024
ROW_TILE = 512
PROJ_COL_CHUNK = 512
RET_STEP = 512
S5_STEP = 128
S5_PANEL = 512
ATTN_TQ = 512


def _params(*semantics):
    return pltpu.CompilerParams(dimension_semantics=semantics,
                                vmem_limit_bytes=V7X_VMEM_LIMIT_BYTES)


def _resident(shape):
    zeros = (0,) * len(shape)
    return pl.BlockSpec(shape, lambda *_: zeros, pipeline_mode=pl.Buffered(1))


def _rms(x, gain):
    return x * lax.rsqrt(jnp.mean(x * x, axis=-1, keepdims=True) + NORM_EPS) * gain


def _sigmoid(x):
    return 1.0 / (1.0 + jnp.exp(-x))


def _dot(a, b):
    return jnp.dot(a, b, preferred_element_type=F32)


def _dot_nt(a, b):
    return lax.dot_general(a, b, (((1,), (1,)), ((), ())), preferred_element_type=F32)


def _dot_tn(a, b):
    return lax.dot_general(a, b, (((0,), (0,)), ((), ())), preferred_element_type=F32)


def _rope_tables(seq, scale):
    inv = ROPE_THETA ** (-jnp.arange(0, HEAD_DIM, 2, dtype=F32) / HEAD_DIM)
    ang = jnp.arange(seq, dtype=F32)[:, None] * inv[None, :]
    reps = PAIR // (HEAD_DIM // 2)
    cos = jnp.tile(jnp.cos(ang), (1, reps))
    sin = jnp.tile(jnp.sin(ang), (1, reps))
    first_half = (jnp.arange(PAIR) % HEAD_DIM) < HEAD_DIM // 2
    sin = jnp.where(first_half[None, :], -sin, sin)
    return cos * scale, sin * scale


def _norm_proj_kernel(x_ref, g_ref, w_ref, cq_ref, sq_ref, ck_ref, sk_ref, *out_refs,
                      n_q, n_k, n_main):
    hn = _rms(x_ref[...], g_ref[...]).astype(BF16)
    chunk = PROJ_COL_CHUNK
    lane = lax.broadcasted_iota(jnp.int32, (1, chunk), 1)
    first_half = (lane % HEAD_DIM) < HEAD_DIM // 2
    reps = chunk // PAIR
    for lo in range(0, w_ref.shape[1], chunk):
        acc = _dot(hn, w_ref[:, lo:lo + chunk])
        if lo < n_q + n_k:
            cos_ref, sin_ref = (cq_ref, sq_ref) if lo < n_q else (ck_ref, sk_ref)
            cos = jnp.concatenate([cos_ref[...]] * reps, axis=1)
            sin = jnp.concatenate([sin_ref[...]] * reps, axis=1)
            rot = jnp.where(first_half,
                            pltpu.roll(acc, chunk - HEAD_DIM // 2, 1),
                            pltpu.roll(acc, HEAD_DIM // 2, 1))
            acc = acc * cos + rot * sin
        if lo < n_main:
            out_refs[0][:, lo:lo + chunk] = acc.astype(BF16)
        else:
            out_refs[1][:, lo - n_main:lo - n_main + chunk] = acc.astype(BF16)


def _norm_proj(x2d, gain, w, batch, seq, *, n_q, n_k, n_main, q_scale):
    tokens, d_model = x2d.shape
    n_total = w.shape[1]
    n_rest = n_total - n_main
    tm = min(ROW_TILE, seq)
    nt = seq // tm
    cq, sq = _rope_tables(seq, q_scale)
    ck, sk = _rope_tables(seq, 1.0)
    table_spec = pl.BlockSpec((tm, PAIR), lambda b, j: (j, 0))
    out_shape = [jax.ShapeDtypeStruct((tokens, n_main), BF16)]
    out_specs = [pl.BlockSpec((tm, n_main), lambda b, j: (b * nt + j, 0))]
    if n_rest:
        out_shape.append(jax.ShapeDtypeStruct((seq, batch * n_rest), BF16))
        out_specs.append(pl.BlockSpec((tm, n_rest), lambda b, j: (j, b)))
    return pl.pallas_call(
        functools.partial(_norm_proj_kernel, n_q=n_q, n_k=n_k, n_main=n_main),
        grid=(batch, nt),
        in_specs=[
            pl.BlockSpec((tm, d_model), lambda b, j: (b * nt + j, 0)),
            _resident((1, d_model)),
            _resident((d_model, n_total)),
            table_spec, table_spec, table_spec, table_spec,
        ],
        out_specs=out_specs,
        out_shape=out_shape,
        compiler_params=_params("parallel", "parallel"),
        name="norm_proj_rope",
    )(x2d, gain.reshape(1, d_model), w.astype(BF16), cq, sq, ck, sk)


def _retention_tables(n_heads):
    c = RET_CHUNK
    gamma = 1.0 - 2.0 ** (-5.0 - jnp.arange(n_heads, dtype=F32))
    log_g = jnp.log(gamma)
    idx = jnp.arange(c, dtype=F32)
    rel = idx[:, None] - idx[None, :]
    intra = jnp.where(rel >= 0, jnp.exp(log_g[:, None, None] * jnp.maximum(rel, 0.0)), 0.0)
    zeta = jnp.exp(log_g[:, None] * (c - 1 - idx))
    xi = jnp.exp(log_g[:, None] * (idx + 1.0))
    chunk_decay = jnp.exp(log_g * c)
    per_lane = lambda t: jnp.repeat(t.T, HEAD_DIM, axis=1)
    head_of = jnp.arange(PAIR) // HEAD_DIM
    same_head = (head_of[:, None] == head_of[None, :]).astype(F32)
    pair_decay = chunk_decay.reshape(n_heads // 2, 2)[:, head_of]
    state_decay = pair_decay[:, :, None] * same_head[None]
    return intra, per_lane(zeta), per_lane(xi), state_decay, same_head


def _retention_kernel(q_ref, k_ref, v_ref, g_ref, intra_ref, zeta_ref, xi_ref,
                      sdec_ref, same_ref, o_ref, state_ref):
    n_pairs = state_ref.shape[0]
    c = RET_CHUNK

    @pl.when(pl.program_id(1) == 0)
    def _():
        state_ref[...] = jnp.zeros(state_ref.shape, F32)

    lane = lax.broadcasted_iota(jnp.int32, (c, PAIR), 1)
    left = lane < HEAD_DIM
    same = same_ref[...]
    head_mean = (same * (1.0 / HEAD_DIM)).astype(BF16)

    def chunk_body(ci, carry):
        rows = pl.ds(pl.multiple_of(ci * c, c), c)
        for p in range(n_pairs):
            cols = slice(p * PAIR, (p + 1) * PAIR)
            qp = q_ref[0, rows, cols]
            kp = k_ref[0, rows, cols]
            vp = v_ref[0, rows, cols]
            zero = jnp.zeros_like(qp)
            s0 = _dot_nt(jnp.where(left, qp, zero), kp) * intra_ref[2 * p]
            s1 = _dot_nt(jnp.where(left, zero, qp), kp) * intra_ref[2 * p + 1]
            intra = jnp.where(left, _dot(s0.astype(BF16), vp), _dot(s1.astype(BF16), vp))
            state = state_ref[p]
            q_dec = (qp.astype(F32) * xi_ref[:, cols]).astype(BF16)
            out = intra + _dot(q_dec, state.astype(BF16))
            k_dec = (kp.astype(F32) * zeta_ref[:, cols]).astype(BF16)
            state_ref[p] = sdec_ref[p] * state + _dot_tn(k_dec, vp) * same
            mu = _dot(out.astype(BF16), head_mean)
            xc = out - mu
            var = _dot((xc * xc).astype(BF16), head_mean)
            y = xc * lax.rsqrt(var + NORM_EPS)
            g = g_ref[0, rows, cols].astype(F32)
            o_ref[0, rows, cols] = (g * _sigmoid(g) * y).astype(BF16)
        return carry

    lax.fori_loop(0, q_ref.shape[1] // c, chunk_body, 0)


def _retention(proj3d, width):
    batch, seq, _ = proj3d.shape
    n_heads = width // HEAD_DIM
    tc = min(RET_STEP, seq)
    intra, zeta, xi, sdec, same = _retention_tables(n_heads)
    col_spec = lambda i: pl.BlockSpec((1, tc, width), lambda b, j: (b, j, i))
    return pl.pallas_call(
        _retention_kernel,
        grid=(batch, seq // tc),
        in_specs=[col_spec(0), col_spec(1), col_spec(2), col_spec(3),
                  _resident(intra.shape), _resident(zeta.shape), _resident(xi.shape),
                  _resident(sdec.shape), _resident(same.shape)],
        out_specs=pl.BlockSpec((1, tc, width), lambda b, j: (b, j, 0)),
        out_shape=jax.ShapeDtypeStruct((batch, seq, width), BF16),
        scratch_shapes=[pltpu.VMEM((n_heads // 2, PAIR, PAIR), F32)],
        compiler_params=_params("parallel", "arbitrary"),
        name="retention",
    )(proj3d, proj3d, proj3d, proj3d, intra, zeta, xi, sdec, same)


def _s5_tables(lam_re, lam_im, b_re, b_im, c_re, c_im, log_step):
    n_groups = lam_re.shape[0]
    n_oct = n_groups // S5_OCTET
    lr, li = lam_re.astype(F32), lam_im.astype(F32)
    delta = jnp.exp(log_step.astype(F32))[:, None]
    mag = jnp.exp(lr * delta)
    bar_re, bar_im = mag * jnp.cos(li * delta), mag * jnp.sin(li * delta)
    den = lr * lr + li * li
    coef_re = ((bar_re - 1.0) * lr + bar_im * li) / den
    coef_im = (bar_im * lr - (bar_re - 1.0) * li) / den
    br, bi = b_re.astype(F32), b_im.astype(F32)
    bbar_re = coef_re[:, :, None] * br - coef_im[:, :, None] * bi
    bbar_im = coef_re[:, :, None] * bi + coef_im[:, :, None] * br
    eye = jnp.eye(S5_OCTET, dtype=F32)

    def expand(t, order):
        t = t.reshape(2, n_oct, S5_OCTET, S5_STATE, S5_GROUP)
        t = t[:, :, :, :, None, :] * eye[None, None, :, None, :, None]
        return jnp.transpose(t, order)

    n_state = 2 * n_groups * S5_STATE
    b_parts = jnp.stack([bbar_re, bbar_im])
    b_mat = expand(b_parts, (4, 5, 0, 1, 2, 3)).reshape(S5_OCTET * S5_GROUP, n_state)
    c_parts = jnp.stack([c_re.astype(F32), -c_im.astype(F32)])
    c_mat = expand(jnp.swapaxes(c_parts, 2, 3), (0, 1, 2, 3, 4, 5)).reshape(
        n_state, S5_OCTET * S5_GROUP)
    a_re = bar_re.reshape(1, n_groups * S5_STATE)
    a_im = bar_im.reshape(1, n_groups * S5_STATE)
    return a_re, a_im, b_mat.astype(BF16), c_mat.astype(BF16)


def _s5_kernel(u_ref, are_ref, aim_ref, b_ref, c_ref, d_ref, wglu_ref, o_ref,
               xs_ref, carry_ref, *, batch):
    n_half = are_ref.shape[1]
    oct_in = S5_OCTET * S5_GROUP
    oct_st = S5_OCTET * S5_STATE
    n_oct = n_half // oct_st
    steps = u_ref.shape[0] // batch

    @pl.when(pl.program_id(0) == 0)
    def _():
        carry_ref[...] = jnp.zeros(carry_ref.shape, F32)

    for part in range(2):
        for o in range(n_oct):
            cols = slice(part * n_half + o * oct_st, part * n_half + (o + 1) * oct_st)
            xs_ref[:, cols] = _dot(u_ref[:, o * oct_in:(o + 1) * oct_in], b_ref[:, cols])

    for lo in range(0, n_half, S5_PANEL):
        re_cols = slice(lo, lo + S5_PANEL)
        im_cols = slice(n_half + lo, n_half + lo + S5_PANEL)
        a_re = jnp.broadcast_to(are_ref[:, re_cols], (batch, S5_PANEL))
        a_im = jnp.broadcast_to(aim_ref[:, re_cols], (batch, S5_PANEL))

        def step(t, x):
            x_re, x_im = x
            rows = pl.ds(pl.multiple_of(t * batch, batch), batch)
            n_re = (a_re * x_re - a_im * x_im) + xs_ref[rows, re_cols]
            n_im = (a_re * x_im + a_im * x_re) + xs_ref[rows, im_cols]
            xs_ref[rows, re_cols] = n_re
            xs_ref[rows, im_cols] = n_im
            return n_re, n_im

        x_re, x_im = lax.fori_loop(0, steps, step,
                                   (carry_ref[:, re_cols], carry_ref[:, im_cols]), unroll=8)
        carry_ref[:, re_cols] = x_re
        carry_ref[:, im_cols] = x_im

    ys = []
    for o in range(n_oct):
        re_cols = slice(o * oct_st, (o + 1) * oct_st)
        im_cols = slice(n_half + o * oct_st, n_half + (o + 1) * oct_st)
        ys.append(_dot(xs_ref[:, re_cols].astype(BF16), c_ref[re_cols, :])
                  + _dot(xs_ref[:, im_cols].astype(BF16), c_ref[im_cols, :]))
    y = jnp.concatenate(ys, axis=1) + d_ref[...] * u_ref[...].astype(F32)
    y = jax.nn.gelu(y)
    o_ref[...] = (y * _sigmoid(_dot(y.astype(BF16), wglu_ref[...]))).astype(BF16)


def _s5(u_tm, batch, a_re, a_im, b_mat, c_mat, d_skip, w_glu):
    rows, width = u_tm.shape
    seq = rows // batch
    ts = min(S5_STEP, seq)
    n_state = b_mat.shape[1]
    return pl.pallas_call(
        functools.partial(_s5_kernel, batch=batch),
        grid=(seq // ts,),
        in_specs=[pl.BlockSpec((ts * batch, width), lambda j: (j, 0)),
                  _resident(a_re.shape), _resident(a_im.shape),
                  _resident(b_mat.shape), _resident(c_mat.shape),
                  _resident((1, width)), _resident(w_glu.shape)],
        out_specs=pl.BlockSpec((ts * batch, width), lambda j: (j, 0)),
        out_shape=jax.ShapeDtypeStruct((rows, width), BF16),
        scratch_shapes=[pltpu.VMEM((ts * batch, n_state), F32),
                        pltpu.VMEM((batch, n_state), F32)],
        compiler_params=_params("arbitrary"),
        name="s5_scan",
    )(u_tm, a_re, a_im, b_mat, c_mat, d_skip.reshape(1, width).astype(F32), w_glu.astype(BF16))


def _diff_attn_kernel(q_ref, k_ref, v_ref, lam_ref, subln_ref, o_ref,
                      vt_ref, m_ref, l_ref, acc_ref):
    tq = q_ref.shape[1]
    tk = tq
    seq = k_ref.shape[1]
    qi = pl.program_id(2)

    @pl.when(qi == 0)
    def _():
        for lo in range(0, seq, tk):
            vt_ref[:, lo:lo + tk] = v_ref[0, lo:lo + tk, :].astype(F32).T.astype(BF16)

    q = q_ref[0]
    lane = lax.broadcasted_iota(jnp.int32, q.shape, 1)
    zero = jnp.zeros_like(q)
    q_maps = (jnp.where(lane < HEAD_DIM, q, zero), jnp.where(lane < HEAD_DIM, zero, q))

    m_ref[...] = jnp.full(m_ref.shape, -jnp.inf, F32)
    l_ref[...] = jnp.zeros(l_ref.shape, F32)
    acc_ref[...] = jnp.zeros(acc_ref.shape, F32)

    def block(kj, masked):
        start = pl.multiple_of(kj * tk, tk)
        k = k_ref[0, pl.ds(start, tk), :]
        v_t = vt_ref[:, pl.ds(start, tk)]
        for i in range(2):
            s = _dot_nt(k, q_maps[i])
            if masked:
                k_pos = lax.broadcasted_iota(jnp.int32, s.shape, 0)
                q_pos = lax.broadcasted_iota(jnp.int32, s.shape, 1)
                s = jnp.where(k_pos <= q_pos, s, -jnp.inf)
            m_prev = m_ref[i]
            m_part = jnp.max(s.reshape(tk // 8, 8, tq), axis=0)
            m_new = jnp.maximum(m_prev, jnp.max(m_part, axis=0, keepdims=True))
            alpha = jnp.exp2(m_prev - m_new)
            p = jnp.exp2(s - m_new)
            l_ref[i] = alpha * l_ref[i] + jnp.sum(p.reshape(tk // 8, 8, tq), axis=0)
            acc_ref[i] = alpha * acc_ref[i] + _dot(v_t, p.astype(BF16))
            m_ref[i] = m_new

    def full_body(kj, carry):
        block(kj, masked=False)
        return carry

    lax.fori_loop(0, qi, full_body, 0)
    block(qi, masked=True)

    lam_p = lam_ref[...]
    lam = (jnp.exp(jnp.sum(lam_p[0:1] * lam_p[1:2], axis=-1, keepdims=True))
           - jnp.exp(jnp.sum(lam_p[2:3] * lam_p[3:4], axis=-1, keepdims=True))
           + LAMBDA_INIT_ODD)
    inv_l = [1.0 / jnp.sum(l_ref[i], axis=0, keepdims=True) for i in range(2)]
    out = acc_ref[0] * inv_l[0] - lam * (acc_ref[1] * inv_l[1])
    ms = jnp.mean(out * out, axis=0, keepdims=True)
    out = out * lax.rsqrt(ms + NORM_EPS) * (subln_ref[...] * (1.0 - LAMBDA_INIT_ODD))
    o_ref[0] = out.T.astype(BF16)


def _diff_attention(qkv3d, n_heads, lam_params, subln):
    batch, seq, _ = qkv3d.shape
    tq = min(ATTN_TQ, seq)
    return pl.pallas_call(
        _diff_attn_kernel,
        grid=(batch, n_heads, seq // tq),
        in_specs=[pl.BlockSpec((1, tq, PAIR), lambda b, h, i: (b, i, h)),
                  pl.BlockSpec((1, seq, PAIR), lambda b, h, i: (b, 0, n_heads + h)),
                  pl.BlockSpec((1, seq, PAIR), lambda b, h, i: (b, 0, 2 * n_heads + h)),
                  _resident(lam_params.shape), _resident((PAIR, 1))],
        out_specs=pl.BlockSpec((1, tq, PAIR), lambda b, h, i: (b, i, h)),
        out_shape=jax.ShapeDtypeStruct((batch, seq, n_heads * PAIR), BF16),
        scratch_shapes=[pltpu.VMEM((PAIR, seq), BF16),
                        pltpu.VMEM((2, 1, tq), F32), pltpu.VMEM((2, 8, tq), F32),
                        pltpu.VMEM((2, PAIR, tq), F32)],
        compiler_params=_params("parallel", "parallel", "arbitrary"),
        name="diff_attention",
    )(qkv3d, qkv3d, qkv3d, lam_params, subln.reshape(PAIR, 1).astype(F32))


def _ffn_chunks(hidden):
    bounds = list(range(0, hidden, 1024)) + [hidden]
    return list(zip(bounds[:-1], bounds[1:]))


def _post_kernel(*refs, n_mix, final):
    h_ref = refs[0]
    mix_refs = refs[1:1 + n_mix]
    (p_ref, wo_ref, nf_ref, wg_ref, wu_ref, wd_ref, np_ref, wpg_ref, wpp_ref,
     fin_ref, o_ref) = refs[1 + n_mix:]
    h = h_ref[...]
    row = 0
    for m_ref in mix_refs:
        width = m_ref.shape[1]
        h = h + _dot(m_ref[...], wo_ref[row:row + width, :])
        row += width
    hn = _rms(h, nf_ref[...]).astype(BF16)
    ffn = jnp.zeros_like(h)
    for lo, hi in _ffn_chunks(wg_ref.shape[1]):
        gate = _dot(hn, wg_ref[:, lo:hi])
        up = _dot(hn, wu_ref[:, lo:hi])
        ffn = ffn + _dot((gate * _sigmoid(gate) * up).astype(BF16), wd_ref[lo:hi, :])
    h = h + ffn
    hp = _rms(h, np_ref[...]).astype(BF16)
    ple_gate = _sigmoid(_dot(hp, wpg_ref[...]))
    h = h + _dot(p_ref[...].astype(BF16), wpp_ref[...]) * ple_gate
    if final:
        h = _rms(h, fin_ref[...])
    o_ref[...] = h


def _post_mixer(h2d, mixes, p2d, w_out, norm_ffn, w_gate, w_up, w_down,
                norm_ple, w_ple_gate, w_ple_proj, final_norm, batch, seq, *, final):
    tokens, d_model = h2d.shape
    tm = min(ROW_TILE, seq)
    nt = seq // tm
    row_spec = lambda width: pl.BlockSpec((tm, width), lambda b, j: (b * nt + j, 0))
    vec = lambda v: v.reshape(1, d_model).astype(F32)
    weights = [w_out.astype(BF16), vec(norm_ffn), w_gate.astype(BF16), w_up.astype(BF16),
               w_down.astype(BF16), vec(norm_ple), w_ple_gate.astype(BF16),
               w_ple_proj.astype(BF16), vec(final_norm)]
    mix_arrays = [m for m, _ in mixes]
    mix_specs = [spec(tm, nt) for _, spec in mixes]
    return pl.pallas_call(
        functools.partial(_post_kernel, n_mix=len(mixes), final=final),
        grid=(batch, nt),
        in_specs=([row_spec(d_model)] + mix_specs + [row_spec(p2d.shape[1])]
                  + [_resident(w.shape) for w in weights]),
        out_specs=row_spec(d_model),
        out_shape=jax.ShapeDtypeStruct((tokens, d_model), F32),
        compiler_params=_params("parallel", "parallel"),
        name="outproj_ffn_ple",
    )(h2d, *mix_arrays, p2d, *weights)


def _batch_major(width):
    return lambda tm, nt: pl.BlockSpec((tm, width), lambda b, j: (b * nt + j, 0))


def _time_major(width):
    return lambda tm, nt: pl.BlockSpec((tm, width), lambda b, j: (j, b))


def kernel(x, p, norm_mix, norm_ffn, norm_ple, ret_s5_w_in, ret_s5_w_out, s5_lambda_re, s5_lambda_im, s5_b_re, s5_b_im, s5_c_re, s5_c_im, s5_d, s5_log_step, s5_w_glu, diff_w_qkv, diff_w_o, diff_lambda_q1, diff_lambda_k1, diff_lambda_q2, diff_lambda_k2, diff_subln, ffn_w_gate, ffn_w_up, ffn_w_down, ple_w_proj, ple_w_gate, final_norm):
    batch, seq, d_model = x.shape
    tokens = batch * seq
    h = x.reshape(tokens, d_model)
    p2d = p.reshape(p.shape[0], tokens, p.shape[-1])

    s5_width = s5_w_glu.shape[-1]
    ret_width = ret_s5_w_in.shape[-1] - s5_width
    ret_width //= 4
    proj, u_tm = _norm_proj(h, norm_mix[0], ret_s5_w_in[0], batch, seq,
                            n_q=ret_width, n_k=ret_width, n_main=4 * ret_width,
                            q_scale=HEAD_DIM ** -0.5)
    ret = _retention(proj.reshape(batch, seq, 4 * ret_width), ret_width)
    a_re, a_im, b_mat, c_mat = _s5_tables(s5_lambda_re[0], s5_lambda_im[0], s5_b_re[0], s5_b_im[0],
                                          s5_c_re[0], s5_c_im[0], s5_log_step[0])
    ssm_tm = _s5(u_tm.reshape(seq * batch, s5_width), batch, a_re, a_im, b_mat, c_mat,
                 s5_d[0], s5_w_glu[0])
    h = _post_mixer(h, [(ret.reshape(tokens, ret_width), _batch_major(ret_width)),
                        (ssm_tm.reshape(seq, batch * s5_width), _time_major(s5_width))],
                    p2d[0], ret_s5_w_out[0], norm_ffn[0], ffn_w_gate[0], ffn_w_up[0],
                    ffn_w_down[0], norm_ple[0], ple_w_gate[0], ple_w_proj[0], final_norm,
                    batch, seq, final=False)

    v_width = diff_w_o.shape[1]
    n_heads = v_width // PAIR
    (qkv,) = _norm_proj(h, norm_mix[1], diff_w_qkv[0], batch, seq,
                        n_q=v_width, n_k=v_width, n_main=3 * v_width,
                        q_scale=HEAD_DIM ** -0.5 * math.log2(math.e))
    lam_params = jnp.stack([diff_lambda_q1[0], diff_lambda_k1[0],
                            diff_lambda_q2[0], diff_lambda_k2[0]]).astype(F32)
    attn = _diff_attention(qkv.reshape(batch, seq, 3 * v_width), n_heads, lam_params, diff_subln[0])
    h = _post_mixer(h, [(attn.reshape(tokens, v_width), _batch_major(v_width))],
                    p2d[1], diff_w_o[0], norm_ffn[1], ffn_w_gate[1], ffn_w_up[1],
                    ffn_w_down[1], norm_ple[1], ple_w_gate[1], ple_w_proj[1], final_norm,
                    batch, seq, final=True)
    return h.reshape(batch, seq, d_model)
```

```python
import functools
import math

import jax
import jax.numpy as jnp
from jax import lax
from jax.experimental import pallas as pl
from jax.experimental.pallas import tpu as pltpu

F32 = jnp.float32
BF16 = jnp.bfloat16

NORM_EPS = 1e-6
ROPE_THETA = 10000.0
HEAD_DIM = 64
PAIR = 2 * HEAD_DIM
RET_CHUNK = 128
S5_GROUP = 16
S5_STATE = 64
S5_OCTET = 8
LAMBDA_INIT_ODD = 0.8 - 0.6 * math.exp(-0.3 * 1)

V7X_VMEM_LIMIT_BYTES = 56 * 1024 * 1024
ROW_TILE = 512
PROJ_COL_CHUNK = 512
RET_STEP = 512
S5_STEP = 128
S5_PANEL = 512
ATTN_Q_TILE = 1024
ATTN_K_BLOCK = 512


def _params(*semantics):
    return pltpu.CompilerParams(dimension_semantics=semantics,
                                vmem_limit_bytes=V7X_VMEM_LIMIT_BYTES)


def _resident(shape):
    zeros = (0,) * len(shape)
    return pl.BlockSpec(shape, lambda *_: zeros, pipeline_mode=pl.Buffered(1))


def _rms(x, gain):
    return x * lax.rsqrt(jnp.mean(x * x, axis=-1, keepdims=True) + NORM_EPS) * gain


def _sigmoid(x):
    return 1.0 / (1.0 + jnp.exp(-x))


def _dot(a, b):
    return jnp.dot(a, b, preferred_element_type=F32)


def _dot_nt(a, b):
    return lax.dot_general(a, b, (((1,), (1,)), ((), ())), preferred_element_type=F32)


def _dot_tn(a, b):
    return lax.dot_general(a, b, (((0,), (0,)), ((), ())), preferred_element_type=F32)


def _rope_tables(seq, scale):
    inv = ROPE_THETA ** (-jnp.arange(0, HEAD_DIM, 2, dtype=F32) / HEAD_DIM)
    ang = jnp.arange(seq, dtype=F32)[:, None] * inv[None, :]
    reps = PAIR // (HEAD_DIM // 2)
    cos = jnp.tile(jnp.cos(ang), (1, reps))
    sin = jnp.tile(jnp.sin(ang), (1, reps))
    first_half = (jnp.arange(PAIR) % HEAD_DIM) < HEAD_DIM // 2
    sin = jnp.where(first_half[None, :], -sin, sin)
    return cos * scale, sin * scale


def _norm_proj_kernel(x_ref, g_ref, w_ref, cq_ref, sq_ref, ck_ref, sk_ref, *out_refs,
                      n_q, n_k, n_main):
    hn = _rms(x_ref[...], g_ref[...]).astype(BF16)
    chunk = PROJ_COL_CHUNK
    lane = lax.broadcasted_iota(jnp.int32, (1, chunk), 1)
    first_half = (lane % HEAD_DIM) < HEAD_DIM // 2
    reps = chunk // PAIR
    for lo in range(0, w_ref.shape[1], chunk):
        acc = _dot(hn, w_ref[:, lo:lo + chunk])
        if lo < n_q + n_k:
            cos_ref, sin_ref = (cq_ref, sq_ref) if lo < n_q else (ck_ref, sk_ref)
            cos = jnp.concatenate([cos_ref[...]] * reps, axis=1)
            sin = jnp.concatenate([sin_ref[...]] * reps, axis=1)
            rot = jnp.where(first_half,
                            pltpu.roll(acc, chunk - HEAD_DIM // 2, 1),
                            pltpu.roll(acc, HEAD_DIM // 2, 1))
            acc = acc * cos + rot * sin
        if lo < n_main:
            out_refs[0][:, lo:lo + chunk] = acc.astype(BF16)
        else:
            out_refs[1][:, lo - n_main:lo - n_main + chunk] = acc.astype(BF16)


def _norm_proj(x2d, gain, w, batch, seq, *, n_q, n_k, n_main, q_scale):
    tokens, d_model = x2d.shape
    n_total = w.shape[1]
    n_rest = n_total - n_main
    tm = min(ROW_TILE, seq)
    nt = seq // tm
    cq, sq = _rope_tables(seq, q_scale)
    ck, sk = _rope_tables(seq, 1.0)
    table_spec = pl.BlockSpec((tm, PAIR), lambda b, j: (j, 0))
    out_shape = [jax.ShapeDtypeStruct((tokens, n_main), BF16)]
    out_specs = [pl.BlockSpec((tm, n_main), lambda b, j: (b * nt + j, 0))]
    if n_rest:
        out_shape.append(jax.ShapeDtypeStruct((seq, batch * n_rest), BF16))
        out_specs.append(pl.BlockSpec((tm, n_rest), lambda b, j: (j, b)))
    return pl.pallas_call(
        functools.partial(_norm_proj_kernel, n_q=n_q, n_k=n_k, n_main=n_main),
        grid=(batch, nt),
        in_specs=[
            pl.BlockSpec((tm, d_model), lambda b, j: (b * nt + j, 0)),
            _resident((1, d_model)),
            _resident((d_model, n_total)),
            table_spec, table_spec, table_spec, table_spec,
        ],
        out_specs=out_specs,
        out_shape=out_shape,
        compiler_params=_params("parallel", "parallel"),
        name="norm_proj_rope",
    )(x2d, gain.reshape(1, d_model), w.astype(BF16), cq, sq, ck, sk)


def _retention_tables(n_heads):
    c = RET_CHUNK
    gamma = 1.0 - 2.0 ** (-5.0 - jnp.arange(n_heads, dtype=F32))
    log_g = jnp.log(gamma)
    idx = jnp.arange(c, dtype=F32)
    rel = idx[:, None] - idx[None, :]
    intra = jnp.where(rel >= 0, jnp.exp(log_g[:, None, None] * jnp.maximum(rel, 0.0)), 0.0)
    zeta = jnp.exp(log_g[:, None] * (c - 1 - idx))
    xi = jnp.exp(log_g[:, None] * (idx + 1.0))
    chunk_decay = jnp.exp(log_g * c)
    per_lane = lambda t: jnp.repeat(t.T, HEAD_DIM, axis=1)
    head_of = jnp.arange(PAIR) // HEAD_DIM
    same_head = (head_of[:, None] == head_of[None, :]).astype(F32)
    pair_decay = chunk_decay.reshape(n_heads // 2, 2)[:, head_of]
    state_decay = pair_decay[:, :, None] * same_head[None]
    return intra, per_lane(zeta), per_lane(xi), state_decay, same_head


def _retention_kernel(q_ref, k_ref, v_ref, g_ref, intra_ref, zeta_ref, xi_ref,
                      sdec_ref, same_ref, o_ref, state_ref):
    n_pairs = state_ref.shape[0]
    c = RET_CHUNK

    @pl.when(pl.program_id(1) == 0)
    def _():
        state_ref[...] = jnp.zeros(state_ref.shape, F32)

    lane = lax.broadcasted_iota(jnp.int32, (c, PAIR), 1)
    left = lane < HEAD_DIM
    same = same_ref[...]
    head_mean = (same * (1.0 / HEAD_DIM)).astype(BF16)

    units = [(p, lo) for p in range(n_pairs) for lo in range(0, q_ref.shape[1], c)]
    blk = lambda ref, p, lo: ref[0, lo:lo + c, p * PAIR:(p + 1) * PAIR]
    zero = jnp.zeros((c, PAIR), BF16)
    scores = {}
    for p, lo in units:
        qp, kp = blk(q_ref, p, lo), blk(k_ref, p, lo)
        scores[p, lo] = (_dot_nt(jnp.where(left, qp, zero), kp),
                         _dot_nt(jnp.where(left, zero, qp), kp))
    kv = {}
    for p, lo in units:
        k_dec = (blk(k_ref, p, lo).astype(F32) * zeta_ref[:, p * PAIR:(p + 1) * PAIR]).astype(BF16)
        kv[p, lo] = _dot_tn(k_dec, blk(v_ref, p, lo)) * same
    intra = {}
    for p, lo in units:
        s0, s1 = scores[p, lo]
        vp = blk(v_ref, p, lo)
        intra[p, lo] = jnp.where(left,
                                 _dot((s0 * intra_ref[2 * p]).astype(BF16), vp),
                                 _dot((s1 * intra_ref[2 * p + 1]).astype(BF16), vp))
    outs = {}
    for p in range(n_pairs):
        state = state_ref[p]
        for lo in range(0, q_ref.shape[1], c):
            q_dec = (blk(q_ref, p, lo).astype(F32) * xi_ref[:, p * PAIR:(p + 1) * PAIR]).astype(BF16)
            outs[p, lo] = intra[p, lo] + _dot(q_dec, state.astype(BF16))
            state = sdec_ref[p] * state + kv[p, lo]
        state_ref[p] = state
    mus = {u: _dot(outs[u].astype(BF16), head_mean) for u in units}
    xcs = {u: outs[u] - mus[u] for u in units}
    variances = {u: _dot((xcs[u] * xcs[u]).astype(BF16), head_mean) for u in units}
    for p, lo in units:
        y = xcs[p, lo] * lax.rsqrt(variances[p, lo] + NORM_EPS)
        g = blk(g_ref, p, lo).astype(F32)
        o_ref[0, lo:lo + c, p * PAIR:(p + 1) * PAIR] = (g * _sigmoid(g) * y).astype(BF16)


def _retention(proj3d, width):
    batch, seq, _ = proj3d.shape
    n_heads = width // HEAD_DIM
    tc = min(RET_STEP, seq)
    intra, zeta, xi, sdec, same = _retention_tables(n_heads)
    col_spec = lambda i: pl.BlockSpec((1, tc, width), lambda b, j: (b, j, i))
    return pl.pallas_call(
        _retention_kernel,
        grid=(batch, seq // tc),
        in_specs=[col_spec(0), col_spec(1), col_spec(2), col_spec(3),
                  _resident(intra.shape), _resident(zeta.shape), _resident(xi.shape),
                  _resident(sdec.shape), _resident(same.shape)],
        out_specs=pl.BlockSpec((1, tc, width), lambda b, j: (b, j, 0)),
        out_shape=jax.ShapeDtypeStruct((batch, seq, width), BF16),
        scratch_shapes=[pltpu.VMEM((n_heads // 2, PAIR, PAIR), F32)],
        compiler_params=_params("parallel", "arbitrary"),
        name="retention",
    )(proj3d, proj3d, proj3d, proj3d, intra, zeta, xi, sdec, same)


def _s5_tables(lam_re, lam_im, b_re, b_im, c_re, c_im, log_step):
    n_groups = lam_re.shape[0]
    n_oct = n_groups // S5_OCTET
    lr, li = lam_re.astype(F32), lam_im.astype(F32)
    delta = jnp.exp(log_step.astype(F32))[:, None]
    mag = jnp.exp(lr * delta)
    bar_re, bar_im = mag * jnp.cos(li * delta), mag * jnp.sin(li * delta)
    den = lr * lr + li * li
    coef_re = ((bar_re - 1.0) * lr + bar_im * li) / den
    coef_im = (bar_im * lr - (bar_re - 1.0) * li) / den
    br, bi = b_re.astype(F32), b_im.astype(F32)
    bbar_re = coef_re[:, :, None] * br - coef_im[:, :, None] * bi
    bbar_im = coef_re[:, :, None] * bi + coef_im[:, :, None] * br
    eye = jnp.eye(S5_OCTET, dtype=F32)

    def expand(t, order):
        t = t.reshape(2, n_oct, S5_OCTET, S5_STATE, S5_GROUP)
        t = t[:, :, :, :, None, :] * eye[None, None, :, None, :, None]
        return jnp.transpose(t, order)

    n_state = 2 * n_groups * S5_STATE
    b_parts = jnp.stack([bbar_re, bbar_im])
    b_mat = expand(b_parts, (4, 5, 0, 1, 2, 3)).reshape(S5_OCTET * S5_GROUP, n_state)
    c_parts = jnp.stack([c_re.astype(F32), -c_im.astype(F32)])
    c_mat = expand(jnp.swapaxes(c_parts, 2, 3), (0, 1, 2, 3, 4, 5)).reshape(
        n_state, S5_OCTET * S5_GROUP)
    a_re = bar_re.reshape(1, n_groups * S5_STATE)
    a_im = bar_im.reshape(1, n_groups * S5_STATE)
    return a_re, a_im, b_mat.astype(BF16), c_mat.astype(BF16)


def _s5_kernel(u_ref, are_ref, aim_ref, b_ref, c_ref, d_ref, wglu_ref, o_ref,
               xs_ref, carry_ref, *, batch):
    n_half = are_ref.shape[1]
    oct_in = S5_OCTET * S5_GROUP
    oct_st = S5_OCTET * S5_STATE
    n_oct = n_half // oct_st
    steps = u_ref.shape[0] // batch

    @pl.when(pl.program_id(0) == 0)
    def _():
        carry_ref[...] = jnp.zeros(carry_ref.shape, F32)

    for part in range(2):
        for o in range(n_oct):
            cols = slice(part * n_half + o * oct_st, part * n_half + (o + 1) * oct_st)
            xs_ref[:, cols] = _dot(u_ref[:, o * oct_in:(o + 1) * oct_in], b_ref[:, cols])

    for lo in range(0, n_half, S5_PANEL):
        re_cols = slice(lo, lo + S5_PANEL)
        im_cols = slice(n_half + lo, n_half + lo + S5_PANEL)
        a_re = jnp.broadcast_to(are_ref[:, re_cols], (batch, S5_PANEL))
        a_im = jnp.broadcast_to(aim_ref[:, re_cols], (batch, S5_PANEL))

        def step(t, x):
            x_re, x_im = x
            rows = pl.ds(pl.multiple_of(t * batch, batch), batch)
            n_re = (a_re * x_re - a_im * x_im) + xs_ref[rows, re_cols]
            n_im = (a_re * x_im + a_im * x_re) + xs_ref[rows, im_cols]
            xs_ref[rows, re_cols] = n_re
            xs_ref[rows, im_cols] = n_im
            return n_re, n_im

        x_re, x_im = lax.fori_loop(0, steps, step,
                                   (carry_ref[:, re_cols], carry_ref[:, im_cols]), unroll=8)
        carry_ref[:, re_cols] = x_re
        carry_ref[:, im_cols] = x_im

    ys = []
    for o in range(n_oct):
        re_cols = slice(o * oct_st, (o + 1) * oct_st)
        im_cols = slice(n_half + o * oct_st, n_half + (o + 1) * oct_st)
        ys.append(_dot(xs_ref[:, re_cols].astype(BF16), c_ref[re_cols, :])
                  + _dot(xs_ref[:, im_cols].astype(BF16), c_ref[im_cols, :]))
    y = jnp.concatenate(ys, axis=1) + d_ref[...] * u_ref[...].astype(F32)
    y = jax.nn.gelu(y)
    o_ref[...] = (y * _sigmoid(_dot(y.astype(BF16), wglu_ref[...]))).astype(BF16)


def _s5(u_tm, batch, a_re, a_im, b_mat, c_mat, d_skip, w_glu):
    rows, width = u_tm.shape
    seq = rows // batch
    ts = min(S5_STEP, seq)
    n_state = b_mat.shape[1]
    return pl.pallas_call(
        functools.partial(_s5_kernel, batch=batch),
        grid=(seq // ts,),
        in_specs=[pl.BlockSpec((ts * batch, width), lambda j: (j, 0)),
                  _resident(a_re.shape), _resident(a_im.shape),
                  _resident(b_mat.shape), _resident(c_mat.shape),
                  _resident((1, width)), _resident(w_glu.shape)],
        out_specs=pl.BlockSpec((ts * batch, width), lambda j: (j, 0)),
        out_shape=jax.ShapeDtypeStruct((rows, width), BF16),
        scratch_shapes=[pltpu.VMEM((ts * batch, n_state), F32),
                        pltpu.VMEM((batch, n_state), F32)],
        compiler_params=_params("arbitrary"),
        name="s5_scan",
    )(u_tm, a_re, a_im, b_mat, c_mat, d_skip.reshape(1, width).astype(F32), w_glu.astype(BF16))


def _diff_attn_kernel(q_ref, k_ref, v_ref, lam_ref, subln_ref, o_ref,
                      vt_ref, qm_ref, sa_ref, sb_ref, m_ref, acc_ref, *, tile):
    tq, tk = tile
    ratio = tq // tk
    seq = k_ref.shape[1]
    n_tiles = seq // tq
    lane = lax.broadcasted_iota(jnp.int32, (tk, PAIR), 1)
    zero = jnp.zeros((tk, PAIR), BF16)
    for lo in range(0, seq, tk):
        vt_ref[0:PAIR, lo:lo + tk] = v_ref[0, lo:lo + tk, :].astype(F32).T.astype(BF16)
        q = q_ref[0, lo:lo + tk, :]
        qm_ref[0, lo:lo + tk, :] = jnp.where(lane < HEAD_DIM, q, zero)
        qm_ref[1, lo:lo + tk, :] = jnp.where(lane < HEAD_DIM, zero, q)
    vt_ref[PAIR:, :] = jnp.ones((vt_ref.shape[0] - PAIR, seq), BF16)

    lam_p = lam_ref[...]
    lam = (jnp.exp(jnp.sum(lam_p[0:1] * lam_p[1:2], axis=-1, keepdims=True))
           - jnp.exp(jnp.sum(lam_p[2:3] * lam_p[3:4], axis=-1, keepdims=True))
           + LAMBDA_INIT_ODD)
    out_gain = subln_ref[...] * (1.0 - LAMBDA_INIT_ODD)

    def scores(qi, kj, i, s_ref, d=0):
        k = k_ref[0, pl.ds(pl.multiple_of(kj * tk, tk), tk), :]
        q = qm_ref[i, pl.ds(pl.multiple_of(qi * tq + d * tk, tk), tq - d * tk), :]
        s_ref[:, d * tk:] = _dot_nt(k, q)

    def consume(kj, i, s_ref, d=None):
        cols = slice(0, tq) if d is None else slice(d * tk, tq)
        s = s_ref[:, cols]
        if d is not None:
            k_pos = lax.broadcasted_iota(jnp.int32, s.shape, 0)
            q_pos = lax.broadcasted_iota(jnp.int32, s.shape, 1)
            s = jnp.where(k_pos <= q_pos, s, -jnp.inf)
        m_prev = m_ref[i, :, cols]
        m_part = jnp.max(s.reshape(tk // 8, 8, s.shape[1]), axis=0)
        m_new = jnp.maximum(m_prev, jnp.max(m_part, axis=0, keepdims=True))
        alpha = jnp.exp2(m_prev - m_new)
        p = jnp.exp2((s - m_new).astype(BF16))
        v_t = vt_ref[:, pl.ds(pl.multiple_of(kj * tk, tk), tk)]
        acc_ref[i, :, cols] = alpha * acc_ref[i, :, cols] + _dot(v_t, p)
        m_ref[i, :, cols] = m_new

    scores(0, 0, 0, sa_ref)

    def tile_body(qi, carry):
        m_ref[...] = jnp.full(m_ref.shape, -jnp.inf, F32)
        acc_ref[...] = jnp.zeros(acc_ref.shape, F32)
        first = qi * ratio

        def full_body(kj, c):
            scores(qi, kj, 1, sb_ref)
            consume(kj, 0, sa_ref)
            scores(qi, kj + 1, 0, sa_ref)
            consume(kj, 1, sb_ref)
            return c

        lax.fori_loop(0, first, full_body, 0)
        for d in range(ratio):
            scores(qi, first + d, 1, sb_ref, d)
            consume(first + d, 0, sa_ref, d)
            if d + 1 < ratio:
                scores(qi, first + d + 1, 0, sa_ref, d + 1)
            else:
                scores(jnp.minimum(qi + 1, n_tiles - 1), 0, 0, sa_ref)
            consume(first + d, 1, sb_ref, d)

        acc0, acc1 = acc_ref[0], acc_ref[1]
        out = (acc0[:PAIR] * (1.0 / acc0[PAIR:PAIR + 1])
               - lam * (acc1[:PAIR] * (1.0 / acc1[PAIR:PAIR + 1])))
        ms = jnp.mean(out * out, axis=0, keepdims=True)
        out = out * lax.rsqrt(ms + NORM_EPS) * out_gain
        o_ref[0, pl.ds(pl.multiple_of(qi * tq, tq), tq), :] = out.T.astype(BF16)
        return carry

    lax.fori_loop(0, n_tiles, tile_body, 0)


def _diff_attention(qkv3d, n_heads, lam_params, subln):
    batch, seq, _ = qkv3d.shape
    tq, tk = min(ATTN_Q_TILE, seq), min(ATTN_K_BLOCK, seq)
    ones_rows = 16
    seq_spec = lambda off: pl.BlockSpec((1, seq, PAIR), lambda b, h: (b, 0, off + h))
    return pl.pallas_call(
        functools.partial(_diff_attn_kernel, tile=(tq, tk)),
        grid=(batch, n_heads),
        in_specs=[seq_spec(0), seq_spec(n_heads), seq_spec(2 * n_heads),
                  _resident(lam_params.shape), _resident((PAIR, 1))],
        out_specs=seq_spec(0),
        out_shape=jax.ShapeDtypeStruct((batch, seq, n_heads * PAIR), BF16),
        scratch_shapes=[pltpu.VMEM((PAIR + ones_rows, seq), BF16),
                        pltpu.VMEM((2, seq, PAIR), BF16),
                        pltpu.VMEM((tk, tq), F32), pltpu.VMEM((tk, tq), F32),
                        pltpu.VMEM((2, 1, tq), F32),
                        pltpu.VMEM((2, PAIR + ones_rows, tq), F32)],
        compiler_params=_params("parallel", "parallel"),
        name="diff_attention",
    )(qkv3d, qkv3d, qkv3d, lam_params, subln.reshape(PAIR, 1).astype(F32))


def _ffn_chunks(hidden):
    bounds = list(range(0, hidden, 1024)) + [hidden]
    return list(zip(bounds[:-1], bounds[1:]))


def _post_kernel(*refs, n_mix, final):
    h_ref = refs[0]
    mix_refs = refs[1:1 + n_mix]
    (p_ref, wo_ref, nf_ref, wg_ref, wu_ref, wd_ref, np_ref, wpg_ref, wpp_ref,
     fin_ref, o_ref) = refs[1 + n_mix:]
    h = h_ref[...]
    row = 0
    for m_ref in mix_refs:
        width = m_ref.shape[1]
        h = h + _dot(m_ref[...], wo_ref[row:row + width, :])
        row += width
    hn = _rms(h, nf_ref[...]).astype(BF16)
    ffn = jnp.zeros_like(h)
    for lo, hi in _ffn_chunks(wg_ref.shape[1]):
        gate = _dot(hn, wg_ref[:, lo:hi])
        up = _dot(hn, wu_ref[:, lo:hi])
        ffn = ffn + _dot((gate * _sigmoid(gate) * up).astype(BF16), wd_ref[lo:hi, :])
    h = h + ffn
    hp = _rms(h, np_ref[...]).astype(BF16)
    ple_gate = _sigmoid(_dot(hp, wpg_ref[...]))
    h = h + _dot(p_ref[...].astype(BF16), wpp_ref[...]) * ple_gate
    if final:
        h = _rms(h, fin_ref[...])
    o_ref[...] = h


def _post_mixer(h2d, mixes, p3d, layer, w_out, norm_ffn, w_gate, w_up, w_down,
                norm_ple, w_ple_gate, w_ple_proj, final_norm, batch, seq, *, final):
    tokens, d_model = h2d.shape
    tm = min(ROW_TILE, seq)
    nt = seq // tm
    row_spec = lambda width: pl.BlockSpec((tm, width), lambda b, j: (b * nt + j, 0))
    p_spec = pl.BlockSpec((None, tm, p3d.shape[2]), lambda b, j: (layer, b * nt + j, 0))
    vec = lambda v: v.reshape(1, d_model).astype(F32)
    weights = [w_out.astype(BF16), vec(norm_ffn), w_gate.astype(BF16), w_up.astype(BF16),
               w_down.astype(BF16), vec(norm_ple), w_ple_gate.astype(BF16),
               w_ple_proj.astype(BF16), vec(final_norm)]
    mix_arrays = [m for m, _ in mixes]
    mix_specs = [spec(tm, nt) for _, spec in mixes]
    return pl.pallas_call(
        functools.partial(_post_kernel, n_mix=len(mixes), final=final),
        grid=(batch, nt),
        in_specs=([row_spec(d_model)] + mix_specs + [p_spec]
                  + [_resident(w.shape) for w in weights]),
        out_specs=row_spec(d_model),
        out_shape=jax.ShapeDtypeStruct((tokens, d_model), F32),
        compiler_params=_params("parallel", "parallel"),
        name="outproj_ffn_ple",
    )(h2d, *mix_arrays, p3d, *weights)


def _batch_major(width):
    return lambda tm, nt: pl.BlockSpec((tm, width), lambda b, j: (b * nt + j, 0))


def _time_major(width):
    return lambda tm, nt: pl.BlockSpec((tm, width), lambda b, j: (j, b))


def kernel(x, p, norm_mix, norm_ffn, norm_ple, ret_s5_w_in, ret_s5_w_out, s5_lambda_re, s5_lambda_im, s5_b_re, s5_b_im, s5_c_re, s5_c_im, s5_d, s5_log_step, s5_w_glu, diff_w_qkv, diff_w_o, diff_lambda_q1, diff_lambda_k1, diff_lambda_q2, diff_lambda_k2, diff_subln, ffn_w_gate, ffn_w_up, ffn_w_down, ple_w_proj, ple_w_gate, final_norm):
    batch, seq, d_model = x.shape
    tokens = batch * seq
    h = x.reshape(tokens, d_model)
    p3d = p.reshape(p.shape[0], tokens, p.shape[-1])

    s5_width = s5_w_glu.shape[-1]
    ret_width = ret_s5_w_in.shape[-1] - s5_width
    ret_width //= 4
    proj, u_tm = _norm_proj(h, norm_mix[0], ret_s5_w_in[0], batch, seq,
                            n_q=ret_width, n_k=ret_width, n_main=4 * ret_width,
                            q_scale=HEAD_DIM ** -0.5)
    ret = _retention(proj.reshape(batch, seq, 4 * ret_width), ret_width)
    a_re, a_im, b_mat, c_mat = _s5_tables(s5_lambda_re[0], s5_lambda_im[0], s5_b_re[0], s5_b_im[0],
                                          s5_c_re[0], s5_c_im[0], s5_log_step[0])
    ssm_tm = _s5(u_tm.reshape(seq * batch, s5_width), batch, a_re, a_im, b_mat, c_mat,
                 s5_d[0], s5_w_glu[0])
    h = _post_mixer(h, [(ret.reshape(tokens, ret_width), _batch_major(ret_width)),
                        (ssm_tm.reshape(seq, batch * s5_width), _time_major(s5_width))],
                    p3d, 0, ret_s5_w_out[0], norm_ffn[0], ffn_w_gate[0], ffn_w_up[0],
                    ffn_w_down[0], norm_ple[0], ple_w_gate[0], ple_w_proj[0], final_norm,
                    batch, seq, final=False)

    v_width = diff_w_o.shape[1]
    n_heads = v_width // PAIR
    (qkv,) = _norm_proj(h, norm_mix[1], diff_w_qkv[0], batch, seq,
                        n_q=v_width, n_k=v_width, n_main=3 * v_width,
                        q_scale=HEAD_DIM ** -0.5 * math.log2(math.e))
    lam_params = jnp.stack([diff_lambda_q1[0], diff_lambda_k1[0],
                            diff_lambda_q2[0], diff_lambda_k2[0]]).astype(F32)
    attn = _diff_attention(qkv.reshape(batch, seq, 3 * v_width), n_heads, lam_params, diff_subln[0])
    h = _post_mixer(h, [(attn.reshape(tokens, v_width), _batch_major(v_width))],
                    p3d, 1, diff_w_o[0], norm_ffn[1], ffn_w_gate[1], ffn_w_up[1],
                    ffn_w_down[1], norm_ple[1], ple_w_gate[1], ple_w_proj[1], final_norm,
                    batch, seq, final=True)
    return h.reshape(batch, seq, d_model)
```

```python
import functools
import math

import jax
import jax.numpy as jnp
from jax import lax
from jax.experimental import pallas as pl
from jax.experimental.pallas import tpu as pltpu

F32 = jnp.float32
BF16 = jnp.bfloat16

NORM_EPS = 1e-6
ROPE_THETA = 10000.0
HEAD_DIM = 64
PAIR = 2 * HEAD_DIM
RET_CHUNK = 128
S5_GROUP = 16
S5_STATE = 64
S5_CHUNK = 16
LANES = 128
LAMBDA_INIT_ODD = 0.8 - 0.6 * math.exp(-0.3 * 1)

V7X_VMEM_LIMIT_BYTES = 56 * 1024 * 1024
ROW_TILE = 512
PROJ_COL_CHUNK = 512
RET_STEP = 512
S5_STEP_CHUNKS = 32
S5_GROUP_BATCH = 8
ATTN_Q_TILE = 1024
ATTN_K_BLOCK = 512


def _params(*semantics):
    return pltpu.CompilerParams(dimension_semantics=semantics,
                                vmem_limit_bytes=V7X_VMEM_LIMIT_BYTES)


def _resident(shape):
    zeros = (0,) * len(shape)
    return pl.BlockSpec(shape, lambda *_: zeros, pipeline_mode=pl.Buffered(1))


def _rms(x, gain):
    return x * lax.rsqrt(jnp.mean(x * x, axis=-1, keepdims=True) + NORM_EPS) * gain


def _sigmoid(x):
    return 1.0 / (1.0 + jnp.exp(-x))


def _dot(a, b):
    return jnp.dot(a, b, preferred_element_type=F32)


def _dot_nt(a, b):
    return lax.dot_general(a, b, (((1,), (1,)), ((), ())), preferred_element_type=F32)


def _dot_tn(a, b):
    return lax.dot_general(a, b, (((0,), (0,)), ((), ())), preferred_element_type=F32)


def _rope_tables(seq, scale):
    inv = ROPE_THETA ** (-jnp.arange(0, HEAD_DIM, 2, dtype=F32) / HEAD_DIM)
    ang = jnp.arange(seq, dtype=F32)[:, None] * inv[None, :]
    reps = PAIR // (HEAD_DIM // 2)
    cos = jnp.tile(jnp.cos(ang), (1, reps))
    sin = jnp.tile(jnp.sin(ang), (1, reps))
    first_half = (jnp.arange(PAIR) % HEAD_DIM) < HEAD_DIM // 2
    sin = jnp.where(first_half[None, :], -sin, sin)
    return cos * scale, sin * scale


def _norm_proj_kernel(x_ref, g_ref, w_ref, cq_ref, sq_ref, ck_ref, sk_ref, *out_refs,
                      n_q, n_k, n_main):
    hn = _rms(x_ref[...], g_ref[...]).astype(BF16)
    chunk = PROJ_COL_CHUNK
    lane = lax.broadcasted_iota(jnp.int32, (1, chunk), 1)
    first_half = (lane % HEAD_DIM) < HEAD_DIM // 2
    reps = chunk // PAIR
    for lo in range(0, w_ref.shape[1], chunk):
        acc = _dot(hn, w_ref[:, lo:lo + chunk])
        if lo < n_q + n_k:
            cos_ref, sin_ref = (cq_ref, sq_ref) if lo < n_q else (ck_ref, sk_ref)
            cos = jnp.concatenate([cos_ref[...]] * reps, axis=1)
            sin = jnp.concatenate([sin_ref[...]] * reps, axis=1)
            rot = jnp.where(first_half,
                            pltpu.roll(acc, chunk - HEAD_DIM // 2, 1),
                            pltpu.roll(acc, HEAD_DIM // 2, 1))
            acc = acc * cos + rot * sin
        if lo < n_main:
            out_refs[0][:, lo:lo + chunk] = acc.astype(BF16)
        else:
            out_refs[1][:, pl.program_id(1), lo - n_main:lo - n_main + chunk] = acc


def _norm_proj(x2d, gain, w, batch, seq, *, n_q, n_k, n_main, q_scale):
    tokens, d_model = x2d.shape
    n_total = w.shape[1]
    n_rest = n_total - n_main
    tm = min(ROW_TILE, seq)
    nt = seq // tm
    cq, sq = _rope_tables(seq, q_scale)
    ck, sk = _rope_tables(seq, 1.0)
    table_spec = pl.BlockSpec((tm, PAIR), lambda j, b: (j, 0))
    out_shape = [jax.ShapeDtypeStruct((tokens, n_main), BF16)]
    out_specs = [pl.BlockSpec((tm, n_main), lambda j, b: (b * nt + j, 0))]
    if n_rest:
        out_shape.append(jax.ShapeDtypeStruct((seq, batch, n_rest), F32))
        out_specs.append(pl.BlockSpec((tm, batch, n_rest), lambda j, b: (j, 0, 0)))
    return pl.pallas_call(
        functools.partial(_norm_proj_kernel, n_q=n_q, n_k=n_k, n_main=n_main),
        grid=(nt, batch),
        in_specs=[
            pl.BlockSpec((tm, d_model), lambda j, b: (b * nt + j, 0)),
            _resident((1, d_model)),
            _resident((d_model, n_total)),
            table_spec, table_spec, table_spec, table_spec,
        ],
        out_specs=out_specs,
        out_shape=out_shape,
        compiler_params=_params("parallel", "arbitrary"),
        name="norm_proj_rope",
    )(x2d, gain.reshape(1, d_model), w.astype(BF16), cq, sq, ck, sk)


def _retention_tables(n_heads):
    c = RET_CHUNK
    gamma = 1.0 - 2.0 ** (-5.0 - jnp.arange(n_heads, dtype=F32))
    log_g = jnp.log(gamma)
    idx = jnp.arange(c, dtype=F32)
    rel = idx[:, None] - idx[None, :]
    intra = jnp.where(rel >= 0, jnp.exp(log_g[:, None, None] * jnp.maximum(rel, 0.0)), 0.0)
    zeta = jnp.exp(log_g[:, None] * (c - 1 - idx))
    xi = jnp.exp(log_g[:, None] * (idx + 1.0))
    chunk_decay = jnp.exp(log_g * c)
    per_lane = lambda t: jnp.repeat(t.T, HEAD_DIM, axis=1)
    head_of = jnp.arange(PAIR) // HEAD_DIM
    same_head = (head_of[:, None] == head_of[None, :]).astype(F32)
    pair_decay = chunk_decay.reshape(n_heads // 2, 2)[:, head_of]
    state_decay = pair_decay[:, :, None] * same_head[None]
    return intra, per_lane(zeta), per_lane(xi), state_decay, same_head


def _retention_kernel(q_ref, k_ref, v_ref, g_ref, intra_ref, zeta_ref, xi_ref,
                      sdec_ref, same_ref, o_ref, state_ref):
    n_pairs = state_ref.shape[0]
    c = RET_CHUNK

    @pl.when(pl.program_id(1) == 0)
    def _():
        state_ref[...] = jnp.zeros(state_ref.shape, F32)

    lane = lax.broadcasted_iota(jnp.int32, (c, PAIR), 1)
    left = lane < HEAD_DIM
    same = same_ref[...]
    head_mean = (same * (1.0 / HEAD_DIM)).astype(BF16)

    units = [(p, lo) for p in range(n_pairs) for lo in range(0, q_ref.shape[1], c)]
    blk = lambda ref, p, lo: ref[0, lo:lo + c, p * PAIR:(p + 1) * PAIR]
    zero = jnp.zeros((c, PAIR), BF16)
    scores = {}
    for p, lo in units:
        qp, kp = blk(q_ref, p, lo), blk(k_ref, p, lo)
        scores[p, lo] = (_dot_nt(jnp.where(left, qp, zero), kp),
                         _dot_nt(jnp.where(left, zero, qp), kp))
    kv = {}
    for p, lo in units:
        k_dec = (blk(k_ref, p, lo).astype(F32) * zeta_ref[:, p * PAIR:(p + 1) * PAIR]).astype(BF16)
        kv[p, lo] = _dot_tn(k_dec, blk(v_ref, p, lo)) * same
    intra = {}
    for p, lo in units:
        s0, s1 = scores[p, lo]
        vp = blk(v_ref, p, lo)
        intra[p, lo] = jnp.where(left,
                                 _dot((s0 * intra_ref[2 * p]).astype(BF16), vp),
                                 _dot((s1 * intra_ref[2 * p + 1]).astype(BF16), vp))
    outs = {}
    for p in range(n_pairs):
        state = state_ref[p]
        for lo in range(0, q_ref.shape[1], c):
            q_dec = (blk(q_ref, p, lo).astype(F32) * xi_ref[:, p * PAIR:(p + 1) * PAIR]).astype(BF16)
            outs[p, lo] = intra[p, lo] + _dot(q_dec, state.astype(BF16))
            state = sdec_ref[p] * state + kv[p, lo]
        state_ref[p] = state
    mus = {u: _dot(outs[u].astype(BF16), head_mean) for u in units}
    xcs = {u: outs[u] - mus[u] for u in units}
    variances = {u: _dot((xcs[u] * xcs[u]).astype(BF16), head_mean) for u in units}
    for p, lo in units:
        y = xcs[p, lo] * lax.rsqrt(variances[p, lo] + NORM_EPS)
        g = blk(g_ref, p, lo).astype(F32)
        o_ref[0, lo:lo + c, p * PAIR:(p + 1) * PAIR] = (g * _sigmoid(g) * y).astype(BF16)


def _retention(proj3d, width):
    batch, seq, _ = proj3d.shape
    n_heads = width // HEAD_DIM
    tc = min(RET_STEP, seq)
    intra, zeta, xi, sdec, same = _retention_tables(n_heads)
    col_spec = lambda i: pl.BlockSpec((1, tc, width), lambda b, j: (b, j, i))
    return pl.pallas_call(
        _retention_kernel,
        grid=(batch, seq // tc),
        in_specs=[col_spec(0), col_spec(1), col_spec(2), col_spec(3),
                  _resident(intra.shape), _resident(zeta.shape), _resident(xi.shape),
                  _resident(sdec.shape), _resident(same.shape)],
        out_specs=pl.BlockSpec((1, tc, width), lambda b, j: (b, j, 0)),
        out_shape=jax.ShapeDtypeStruct((batch, seq, width), BF16),
        scratch_shapes=[pltpu.VMEM((n_heads // 2, PAIR, PAIR), F32)],
        compiler_params=_params("parallel", "arbitrary"),
        name="retention",
    )(proj3d, proj3d, proj3d, proj3d, intra, zeta, xi, sdec, same)


def _s5_tables(lam_re, lam_im, b_re, b_im, c_re, c_im, log_step):
    n_groups, n_state = lam_re.shape
    n_chan = b_re.shape[-1]
    L = S5_CHUNK
    lr, li = lam_re.astype(F32), lam_im.astype(F32)
    delta = jnp.exp(log_step.astype(F32))[:, None]
    steps = jnp.arange(L + 1, dtype=F32)[:, None, None]
    mag = jnp.exp(steps * (lr * delta))
    pow_re = mag * jnp.cos(steps * (li * delta))
    pow_im = mag * jnp.sin(steps * (li * delta))
    bar_re, bar_im = pow_re[1], pow_im[1]
    den = lr * lr + li * li
    coef_re = ((bar_re - 1.0) * lr + bar_im * li) / den
    coef_im = (bar_im * lr - (bar_re - 1.0) * li) / den
    br, bi = b_re.astype(F32), b_im.astype(F32)
    bbar_re = coef_re[:, :, None] * br - coef_im[:, :, None] * bi
    bbar_im = coef_re[:, :, None] * bi + coef_im[:, :, None] * br
    cr, ci = c_re.astype(F32), c_im.astype(F32)
    cp_re = cr[None] * pow_re[:, :, None, :] - ci[None] * pow_im[:, :, None, :]
    cp_im = cr[None] * pow_im[:, :, None, :] + ci[None] * pow_re[:, :, None, :]
    kern = (jnp.einsum('kgcp,gpd->kgdc', cp_re, bbar_re)
            - jnp.einsum('kgcp,gpd->kgdc', cp_im, bbar_im))
    src = jnp.arange(L)[:, None]
    dst = jnp.arange(L)[None, :]
    lag = dst - src
    toe = jnp.where((lag >= 0)[:, :, None, None, None], kern[jnp.maximum(lag, 0)], 0.0)
    toe = jnp.transpose(toe, (2, 0, 3, 1, 4)).reshape(n_groups, L * n_chan, L * n_chan)
    dec_re, dec_im = pow_re[L - 1 - jnp.arange(L)], pow_im[L - 1 - jnp.arange(L)]
    e_re = dec_re[..., None] * bbar_re[None] - dec_im[..., None] * bbar_im[None]
    e_im = dec_re[..., None] * bbar_im[None] + dec_im[..., None] * bbar_re[None]
    to_rows = lambda e: jnp.transpose(e, (1, 0, 3, 2)).reshape(n_groups, L * n_chan, n_state)
    e_re, e_im = to_rows(e_re), to_rows(e_im)
    b_pow = jnp.concatenate([e_re, e_im, e_im, e_re], axis=-1)
    from_state = lambda m: jnp.transpose(m[1:], (1, 3, 0, 2)).reshape(n_groups, n_state, L * n_chan)
    c_pow = jnp.concatenate([from_state(cp_re), -from_state(cp_im)], axis=1)
    a1 = jnp.concatenate([pow_re[L], pow_re[L]], axis=-1).reshape(1, n_groups * 2 * n_state)
    a2 = jnp.concatenate([-pow_im[L], pow_im[L]], axis=-1).reshape(1, n_groups * 2 * n_state)
    return toe.astype(BF16), b_pow.astype(BF16), c_pow.astype(BF16), a1, a2


def _granule_transpose(x):
    n = len(x)
    half = n // 2
    lane = lax.broadcasted_iota(jnp.int32, (1, LANES), 1)

    def rotate(cols, g):
        c0, c1 = cols
        if g >= half:
            c0, c1, g = c1, c0, g - half
        if g == 0:
            return [c0, c1]
        s = g * S5_GROUP
        r0, r1 = pltpu.roll(c0, s, 1), pltpu.roll(c1, s, 1)
        wrapped = lane < s
        return [jnp.where(wrapped, r1, r0), jnp.where(wrapped, r0, r1)]

    w = [rotate(x[i], i) for i in range(n)]
    for bit in range(3):
        d = 1 << bit
        take = ((lane // S5_GROUP) >> bit) & 1 == 1
        w = [[jnp.where(take, w[(j + d) % n][col], w[j][col]) for col in range(2)]
             for j in range(n)]
    w = [[w[j][0], w[(j + half) % n][1]] for j in range(n)]
    return [rotate(w[(-j) % n], (-j) % n) for j in range(n)]


def _s5_kernel(u_ref, toe_ref, bpow_ref, cpow_ref, a1_ref, a2_ref, d_ref, wglu_ref, o_ref,
               carry_ref, *, batch):
    n_chunks = u_ref.shape[0]
    rows = n_chunks * batch
    width = u_ref.shape[2]
    st = 2 * S5_STATE
    fold = S5_CHUNK * S5_GROUP

    @pl.when(pl.program_id(0) == 0)
    def _():
        carry_ref[...] = jnp.zeros(carry_ref.shape, F32)

    def slab(t):
        return u_ref[:, t * batch:(t + 1) * batch, :].reshape(rows, width)

    n_half = fold // S5_GROUP
    y_cols = [[None] * (width // LANES) for _ in range(S5_CHUNK)]
    for half in range(width // fold):
        base = half * fold
        x = _granule_transpose(
            [[slab(t)[:, base + c * LANES: base + (c + 1) * LANES] for c in range(2)]
             for t in range(S5_CHUNK)])
        y_half = []
        for j0 in range(0, n_half, S5_GROUP_BATCH):
            groups = range(half * n_half + j0, half * n_half + j0 + S5_GROUP_BATCH)
            folded = {g: jnp.concatenate(x[g % n_half], axis=1).astype(BF16) for g in groups}
            x_end = {g: _dot(folded[g], bpow_ref[g]) for g in groups}
            x_start = {}
            for g in groups:
                lanes = slice(g * st, (g + 1) * st)
                a1, a2 = a1_ref[:, lanes], a2_ref[:, lanes]
                xs, xw = carry_ref[0, :, lanes], carry_ref[1, :, lanes]
                starts = []
                for ch in range(n_chunks):
                    starts.append(xs)
                    e = x_end[g][ch * batch:(ch + 1) * batch]
                    xs, xw = a1 * xs + a2 * xw + e[:, :st], a1 * xw - a2 * xs + e[:, st:]
                carry_ref[0, :, lanes] = xs
                carry_ref[1, :, lanes] = xw
                x_start[g] = jnp.concatenate(starts, axis=0).astype(BF16)
            for g in groups:
                y = _dot(folded[g], toe_ref[g]) + _dot(x_start[g], cpow_ref[g])
                y_half.append([y[:, :LANES], y[:, LANES:]])
        y_t = _granule_transpose(y_half)
        for t in range(S5_CHUNK):
            y_cols[t][2 * half], y_cols[t][2 * half + 1] = y_t[t]

    for t in range(0, S5_CHUNK, 2):
        outs = []
        for tt in (t, t + 1):
            y = jnp.concatenate(y_cols[tt], axis=1) + d_ref[...] * slab(tt)
            y = jax.nn.gelu(y)
            z = y * _sigmoid(_dot(y.astype(BF16), wglu_ref[...]))
            outs.append(z.reshape(n_chunks, batch, width))
        o_ref[:, t * batch:(t + 2) * batch, :] = jnp.concatenate(outs, axis=1).astype(BF16)


def _s5(u_tm, batch, toe, b_pow, c_pow, a1, a2, d_skip, w_glu):
    seq, _, width = u_tm.shape
    span = S5_CHUNK * batch
    n_chunks = min(S5_STEP_CHUNKS, seq // S5_CHUNK)
    total = seq // S5_CHUNK
    u3 = u_tm.reshape(total, span, width)
    out = pl.pallas_call(
        functools.partial(_s5_kernel, batch=batch),
        grid=(total // n_chunks,),
        in_specs=[pl.BlockSpec((n_chunks, span, width), lambda j: (j, 0, 0)),
                  _resident(toe.shape), _resident(b_pow.shape), _resident(c_pow.shape),
                  _resident(a1.shape), _resident(a2.shape),
                  _resident((1, width)), _resident(w_glu.shape)],
        out_specs=pl.BlockSpec((n_chunks, span, width), lambda j: (j, 0, 0)),
        out_shape=jax.ShapeDtypeStruct((total, span, width), BF16),
        scratch_shapes=[pltpu.VMEM((2, batch, a1.shape[1]), F32)],
        compiler_params=_params("arbitrary"),
        name="s5_chunked",
    )(u3, toe, b_pow, c_pow, a1, a2, d_skip.reshape(1, width).astype(F32), w_glu.astype(BF16))
    return out.reshape(seq * batch, width)


def _diff_attn_kernel(q_ref, k_ref, v_ref, lam_ref, subln_ref, o_ref,
                      vt_ref, qm_ref, sa_ref, sb_ref, m_ref, acc_ref, *, tile):
    tq, tk = tile
    ratio = tq // tk
    seq = k_ref.shape[1]
    n_tiles = seq // tq
    lane = lax.broadcasted_iota(jnp.int32, (tk, PAIR), 1)
    zero = jnp.zeros((tk, PAIR), BF16)
    for lo in range(0, seq, tk):
        vt_ref[0:PAIR, lo:lo + tk] = v_ref[0, lo:lo + tk, :].astype(F32).T.astype(BF16)
        q = q_ref[0, lo:lo + tk, :]
        qm_ref[0, lo:lo + tk, :] = jnp.where(lane < HEAD_DIM, q, zero)
        qm_ref[1, lo:lo + tk, :] = jnp.where(lane < HEAD_DIM, zero, q)
    vt_ref[PAIR:, :] = jnp.ones((vt_ref.shape[0] - PAIR, seq), BF16)

    lam_p = lam_ref[...]
    lam = (jnp.exp(jnp.sum(lam_p[0:1] * lam_p[1:2], axis=-1, keepdims=True))
           - jnp.exp(jnp.sum(lam_p[2:3] * lam_p[3:4], axis=-1, keepdims=True))
           + LAMBDA_INIT_ODD)
    out_gain = subln_ref[...] * (1.0 - LAMBDA_INIT_ODD)

    def scores(qi, kj, i, s_ref, d=0):
        k = k_ref[0, pl.ds(pl.multiple_of(kj * tk, tk), tk), :]
        q = qm_ref[i, pl.ds(pl.multiple_of(qi * tq + d * tk, tk), tq - d * tk), :]
        s_ref[:, d * tk:] = _dot_nt(k, q)

    def consume(kj, i, s_ref, d=None):
        cols = slice(0, tq) if d is None else slice(d * tk, tq)
        s = s_ref[:, cols]
        if d is not None:
            k_pos = lax.broadcasted_iota(jnp.int32, s.shape, 0)
            q_pos = lax.broadcasted_iota(jnp.int32, s.shape, 1)
            s = jnp.where(k_pos <= q_pos, s, -jnp.inf)
        m_prev = m_ref[i, :, cols]
        m_part = jnp.max(s.reshape(tk // 8, 8, s.shape[1]), axis=0)
        m_new = jnp.maximum(m_prev, jnp.max(m_part, axis=0, keepdims=True))
        alpha = jnp.exp2(m_prev - m_new)
        p = jnp.exp2((s - m_new).astype(BF16))
        v_t = vt_ref[:, pl.ds(pl.multiple_of(kj * tk, tk), tk)]
        acc_ref[i, :, cols] = alpha * acc_ref[i, :, cols] + _dot(v_t, p)
        m_ref[i, :, cols] = m_new

    scores(0, 0, 0, sa_ref)

    def tile_body(qi, carry):
        m_ref[...] = jnp.full(m_ref.shape, -jnp.inf, F32)
        acc_ref[...] = jnp.zeros(acc_ref.shape, F32)
        first = qi * ratio

        def full_body(kj, c):
            scores(qi, kj, 1, sb_ref)
            consume(kj, 0, sa_ref)
            scores(qi, kj + 1, 0, sa_ref)
            consume(kj, 1, sb_ref)
            return c

        lax.fori_loop(0, first, full_body, 0)
        for d in range(ratio):
            scores(qi, first + d, 1, sb_ref, d)
            consume(first + d, 0, sa_ref, d)
            if d + 1 < ratio:
                scores(qi, first + d + 1, 0, sa_ref, d + 1)
            else:
                scores(jnp.minimum(qi + 1, n_tiles - 1), 0, 0, sa_ref)
            consume(first + d, 1, sb_ref, d)

        acc0, acc1 = acc_ref[0], acc_ref[1]
        out = (acc0[:PAIR] * (1.0 / acc0[PAIR:PAIR + 1])
               - lam * (acc1[:PAIR] * (1.0 / acc1[PAIR:PAIR + 1])))
        ms = jnp.mean(out * out, axis=0, keepdims=True)
        out = out * lax.rsqrt(ms + NORM_EPS) * out_gain
        o_ref[0, pl.ds(pl.multiple_of(qi * tq, tq), tq), :] = out.T.astype(BF16)
        return carry

    lax.fori_loop(0, n_tiles, tile_body, 0)


def _diff_attention(qkv3d, n_heads, lam_params, subln):
    batch, seq, _ = qkv3d.shape
    tq, tk = min(ATTN_Q_TILE, seq), min(ATTN_K_BLOCK, seq)
    ones_rows = 16
    seq_spec = lambda off: pl.BlockSpec((1, seq, PAIR), lambda b, h: (b, 0, off + h))
    return pl.pallas_call(
        functools.partial(_diff_attn_kernel, tile=(tq, tk)),
        grid=(batch, n_heads),
        in_specs=[seq_spec(0), seq_spec(n_heads), seq_spec(2 * n_heads),
                  _resident(lam_params.shape), _resident((PAIR, 1))],
        out_specs=seq_spec(0),
        out_shape=jax.ShapeDtypeStruct((batch, seq, n_heads * PAIR), BF16),
        scratch_shapes=[pltpu.VMEM((PAIR + ones_rows, seq), BF16),
                        pltpu.VMEM((2, seq, PAIR), BF16),
                        pltpu.VMEM((tk, tq), F32), pltpu.VMEM((tk, tq), F32),
                        pltpu.VMEM((2, 1, tq), F32),
                        pltpu.VMEM((2, PAIR + ones_rows, tq), F32)],
        compiler_params=_params("parallel", "parallel"),
        name="diff_attention",
    )(qkv3d, qkv3d, qkv3d, lam_params, subln.reshape(PAIR, 1).astype(F32))


def _ffn_chunks(hidden):
    bounds = list(range(0, hidden, 1024)) + [hidden]
    return list(zip(bounds[:-1], bounds[1:]))


def _post_kernel(*refs, time_major, final):
    n_mix = len(time_major)
    h_ref = refs[0]
    mix_refs = refs[1:1 + n_mix]
    rest = list(refs[1 + n_mix:])
    perm_ref = rest.pop(1) if any(time_major) else None
    (p_ref, wo_ref, nf_ref, wg_ref, wu_ref, wd_ref, np_ref, wpg_ref, wpp_ref,
     fin_ref, o_ref) = rest
    rows = h_ref.shape[0] * h_ref.shape[1]
    flat = lambda ref: ref[...].reshape(rows, ref.shape[-1])
    h = flat(h_ref)
    row = 0
    for m_ref, is_time_major in zip(mix_refs, time_major):
        mix = flat(m_ref)
        if is_time_major:
            mix = _dot(perm_ref[...], mix).astype(BF16)
        width = mix.shape[1]
        h = h + _dot(mix, wo_ref[row:row + width, :])
        row += width
    hn = _rms(h, nf_ref[...]).astype(BF16)
    ffn = jnp.zeros_like(h)
    for lo, hi in _ffn_chunks(wg_ref.shape[1]):
        gate = _dot(hn, wg_ref[:, lo:hi])
        up = _dot(hn, wu_ref[:, lo:hi])
        ffn = ffn + _dot((gate * _sigmoid(gate) * up).astype(BF16), wd_ref[lo:hi, :])
    h = h + ffn
    hp = _rms(h, np_ref[...]).astype(BF16)
    ple_gate = _sigmoid(_dot(hp, wpg_ref[...]))
    h = h + _dot(flat(p_ref).astype(BF16), wpp_ref[...]) * ple_gate
    if final:
        h = _rms(h, fin_ref[...])
    o_ref[...] = h.reshape(o_ref.shape)


def _post_mixer(h3d, mixes, p4d, layer, w_out, norm_ffn, w_gate, w_up, w_down,
                norm_ple, w_ple_gate, w_ple_proj, final_norm, *, final):
    batch, seq, d_model = h3d.shape
    tt = min(ROW_TILE // batch, seq)
    rows = tt * batch
    bm_spec = lambda width: pl.BlockSpec((batch, tt, width), lambda j: (0, j, 0))
    tm_spec = lambda width: pl.BlockSpec((1, rows, width), lambda j: (j, 0, 0))
    p_spec = pl.BlockSpec((None, batch, tt, p4d.shape[3]), lambda j: (layer, 0, j, 0))
    vec = lambda v: v.reshape(1, d_model).astype(F32)
    weights = [w_out.astype(BF16), vec(norm_ffn), w_gate.astype(BF16), w_up.astype(BF16),
               w_down.astype(BF16), vec(norm_ple), w_ple_gate.astype(BF16),
               w_ple_proj.astype(BF16), vec(final_norm)]
    time_major = tuple(tm for _, tm in mixes)
    mix_arrays = [m.reshape(seq // tt, rows, m.shape[-1]) if tm else m for m, tm in mixes]
    mix_specs = [(tm_spec if tm else bm_spec)(m.shape[-1]) for m, tm in mixes]
    extra, extra_specs = [], []
    if any(time_major):
        dst = jnp.arange(rows)
        src = (dst % tt) * batch + dst // tt
        extra = [(src[:, None] == jnp.arange(rows)[None, :]).astype(BF16)]
        extra_specs = [_resident((rows, rows))]
    return pl.pallas_call(
        functools.partial(_post_kernel, time_major=time_major, final=final),
        grid=(seq // tt,),
        in_specs=([bm_spec(d_model)] + mix_specs + [p_spec] + extra_specs
                  + [_resident(w.shape) for w in weights]),
        out_specs=bm_spec(d_model),
        out_shape=jax.ShapeDtypeStruct((batch, seq, d_model), F32),
        compiler_params=_params("parallel"),
        name="outproj_ffn_ple",
    )(h3d, *mix_arrays, p4d, *extra, *weights)


def kernel(x, p, norm_mix, norm_ffn, norm_ple, ret_s5_w_in, ret_s5_w_out, s5_lambda_re, s5_lambda_im, s5_b_re, s5_b_im, s5_c_re, s5_c_im, s5_d, s5_log_step, s5_w_glu, diff_w_qkv, diff_w_o, diff_lambda_q1, diff_lambda_k1, diff_lambda_q2, diff_lambda_k2, diff_subln, ffn_w_gate, ffn_w_up, ffn_w_down, ple_w_proj, ple_w_gate, final_norm):
    batch, seq, d_model = x.shape
    tokens = batch * seq
    p4d = p.reshape(p.shape[0], batch, seq, p.shape[-1])

    s5_width = s5_w_glu.shape[-1]
    ret_width = ret_s5_w_in.shape[-1] - s5_width
    ret_width //= 4
    proj, u_tm = _norm_proj(x.reshape(tokens, d_model), norm_mix[0], ret_s5_w_in[0], batch, seq,
                            n_q=ret_width, n_k=ret_width, n_main=4 * ret_width,
                            q_scale=HEAD_DIM ** -0.5)
    ret = _retention(proj.reshape(batch, seq, 4 * ret_width), ret_width)
    toe, b_pow, c_pow, a1, a2 = _s5_tables(s5_lambda_re[0], s5_lambda_im[0], s5_b_re[0], s5_b_im[0],
                                           s5_c_re[0], s5_c_im[0], s5_log_step[0])
    ssm_tm = _s5(u_tm, batch, toe, b_pow, c_pow, a1, a2, s5_d[0], s5_w_glu[0])
    h = _post_mixer(x, [(ret, False), (ssm_tm, True)],
                    p4d, 0, ret_s5_w_out[0], norm_ffn[0], ffn_w_gate[0], ffn_w_up[0],
                    ffn_w_down[0], norm_ple[0], ple_w_gate[0], ple_w_proj[0], final_norm,
                    final=False)

    v_width = diff_w_o.shape[1]
    n_heads = v_width // PAIR
    (qkv,) = _norm_proj(h.reshape(tokens, d_model), norm_mix[1], diff_w_qkv[0], batch, seq,
                        n_q=v_width, n_k=v_width, n_main=3 * v_width,
                        q_scale=HEAD_DIM ** -0.5 * math.log2(math.e))
    lam_params = jnp.stack([diff_lambda_q1[0], diff_lambda_k1[0],
                            diff_lambda_q2[0], diff_lambda_k2[0]]).astype(F32)
    attn = _diff_attention(qkv.reshape(batch, seq, 3 * v_width), n_heads, lam_params, diff_subln[0])
    return _post_mixer(h, [(attn, False)],
                       p4d, 1, diff_w_o[0], norm_ffn[1], ffn_w_gate[1], ffn_w_up[1],
                       ffn_w_down[1], norm_ple[1], ple_w_gate[1], ple_w_proj[1], final_norm,
                       final=True)
```

```python
import functools
import math

import jax
import jax.numpy as jnp
from jax import lax
from jax.experimental import pallas as pl
from jax.experimental.pallas import tpu as pltpu

F32 = jnp.float32
BF16 = jnp.bfloat16

NORM_EPS = 1e-6
ROPE_THETA = 10000.0
HEAD_DIM = 64
PAIR = 2 * HEAD_DIM
RET_CHUNK = 128
S5_GROUP = 16
S5_STATE = 64
S5_CHUNK = 16
LANES = 128
LAMBDA_INIT_ODD = 0.8 - 0.6 * math.exp(-0.3 * 1)

V7X_VMEM_LIMIT_BYTES = 56 * 1024 * 1024
ROW_TILE = 512
PROJ_COL_CHUNK = 512
RET_STEP = 512
S5_STEP_CHUNKS = 32
S5_GROUP_BATCH = 8
ATTN_Q_TILE = 1024
ATTN_K_BLOCK = 512


def _params(*semantics):
    return pltpu.CompilerParams(dimension_semantics=semantics,
                                vmem_limit_bytes=V7X_VMEM_LIMIT_BYTES)


def _resident(shape):
    zeros = (0,) * len(shape)
    return pl.BlockSpec(shape, lambda *_: zeros, pipeline_mode=pl.Buffered(1))


def _rms(x, gain):
    return x * lax.rsqrt(jnp.mean(x * x, axis=-1, keepdims=True) + NORM_EPS) * gain


def _sigmoid(x):
    return 1.0 / (1.0 + jnp.exp(-x))


def _dot(a, b):
    return jnp.dot(a, b, preferred_element_type=F32)


def _dot_nt(a, b):
    return lax.dot_general(a, b, (((1,), (1,)), ((), ())), preferred_element_type=F32)


def _dot_tn(a, b):
    return lax.dot_general(a, b, (((0,), (0,)), ((), ())), preferred_element_type=F32)


def _rope_tables(seq, scale):
    inv = ROPE_THETA ** (-jnp.arange(0, HEAD_DIM, 2, dtype=F32) / HEAD_DIM)
    ang = jnp.arange(seq, dtype=F32)[:, None] * inv[None, :]
    reps = PAIR // (HEAD_DIM // 2)
    cos = jnp.tile(jnp.cos(ang), (1, reps))
    sin = jnp.tile(jnp.sin(ang), (1, reps))
    first_half = (jnp.arange(PAIR) % HEAD_DIM) < HEAD_DIM // 2
    sin = jnp.where(first_half[None, :], -sin, sin)
    return cos * scale, sin * scale


def _norm_proj_kernel(x_ref, g_ref, w_ref, cq_ref, sq_ref, ck_ref, sk_ref, *out_refs,
                      n_q, n_k, n_main):
    hn = _rms(x_ref[...], g_ref[...]).astype(BF16)
    chunk = PROJ_COL_CHUNK
    lane = lax.broadcasted_iota(jnp.int32, (1, chunk), 1)
    first_half = (lane % HEAD_DIM) < HEAD_DIM // 2
    reps = chunk // PAIR
    for lo in range(0, w_ref.shape[1], chunk):
        acc = _dot(hn, w_ref[:, lo:lo + chunk])
        if lo < n_q + n_k:
            cos_ref, sin_ref = (cq_ref, sq_ref) if lo < n_q else (ck_ref, sk_ref)
            cos = jnp.concatenate([cos_ref[...]] * reps, axis=1)
            sin = jnp.concatenate([sin_ref[...]] * reps, axis=1)
            rot = jnp.where(first_half,
                            pltpu.roll(acc, chunk - HEAD_DIM // 2, 1),
                            pltpu.roll(acc, HEAD_DIM // 2, 1))
            acc = acc * cos + rot * sin
        if lo < n_main:
            out_refs[0][:, lo:lo + chunk] = acc.astype(BF16)
        else:
            out_refs[1][:, pl.program_id(1), lo - n_main:lo - n_main + chunk] = acc


def _norm_proj(x2d, gain, w, batch, seq, *, n_q, n_k, n_main, q_scale):
    tokens, d_model = x2d.shape
    n_total = w.shape[1]
    n_rest = n_total - n_main
    tm = min(ROW_TILE, seq)
    nt = seq // tm
    cq, sq = _rope_tables(seq, q_scale)
    ck, sk = _rope_tables(seq, 1.0)
    table_spec = pl.BlockSpec((tm, PAIR), lambda j, b: (j, 0))
    out_shape = [jax.ShapeDtypeStruct((tokens, n_main), BF16)]
    out_specs = [pl.BlockSpec((tm, n_main), lambda j, b: (b * nt + j, 0))]
    if n_rest:
        out_shape.append(jax.ShapeDtypeStruct((seq, batch, n_rest), F32))
        out_specs.append(pl.BlockSpec((tm, batch, n_rest), lambda j, b: (j, 0, 0)))
    return pl.pallas_call(
        functools.partial(_norm_proj_kernel, n_q=n_q, n_k=n_k, n_main=n_main),
        grid=(nt, batch),
        in_specs=[
            pl.BlockSpec((tm, d_model), lambda j, b: (b * nt + j, 0)),
            _resident((1, d_model)),
            _resident((d_model, n_total)),
            table_spec, table_spec, table_spec, table_spec,
        ],
        out_specs=out_specs,
        out_shape=out_shape,
        compiler_params=_params("parallel", "arbitrary"),
        name="norm_proj_rope",
    )(x2d, gain.reshape(1, d_model), w.astype(BF16), cq, sq, ck, sk)


def _retention_tables(n_heads):
    c = RET_CHUNK
    gamma = 1.0 - 2.0 ** (-5.0 - jnp.arange(n_heads, dtype=F32))
    log_g = jnp.log(gamma)
    idx = jnp.arange(c, dtype=F32)
    rel = idx[:, None] - idx[None, :]
    intra = jnp.where(rel >= 0, jnp.exp(log_g[:, None, None] * jnp.maximum(rel, 0.0)), 0.0)
    zeta = jnp.exp(log_g[:, None] * (c - 1 - idx))
    xi = jnp.exp(log_g[:, None] * (idx + 1.0))
    chunk_decay = jnp.exp(log_g * c)
    per_lane = lambda t: jnp.repeat(t.T, HEAD_DIM, axis=1)
    head_of = jnp.arange(PAIR) // HEAD_DIM
    same_head = (head_of[:, None] == head_of[None, :]).astype(F32)
    pair_decay = chunk_decay.reshape(n_heads // 2, 2)[:, head_of]
    state_decay = pair_decay[:, :, None] * same_head[None]
    return intra, per_lane(zeta), per_lane(xi), state_decay, same_head


def _retention_kernel(q_ref, k_ref, v_ref, g_ref, intra_ref, zeta_ref, xi_ref,
                      sdec_ref, same_ref, o_ref, state_ref):
    n_pairs = state_ref.shape[0]
    c = RET_CHUNK

    @pl.when(pl.program_id(1) == 0)
    def _():
        state_ref[...] = jnp.zeros(state_ref.shape, F32)

    lane = lax.broadcasted_iota(jnp.int32, (c, PAIR), 1)
    left = lane < HEAD_DIM
    same = same_ref[...]
    head_mean = (same * (1.0 / HEAD_DIM)).astype(BF16)

    units = [(p, lo) for p in range(n_pairs) for lo in range(0, q_ref.shape[1], c)]
    blk = lambda ref, p, lo: ref[0, lo:lo + c, p * PAIR:(p + 1) * PAIR]
    zero = jnp.zeros((c, PAIR), BF16)
    scores = {}
    for p, lo in units:
        qp, kp = blk(q_ref, p, lo), blk(k_ref, p, lo)
        scores[p, lo] = (_dot_nt(jnp.where(left, qp, zero), kp),
                         _dot_nt(jnp.where(left, zero, qp), kp))
    kv = {}
    for p, lo in units:
        k_dec = (blk(k_ref, p, lo).astype(F32) * zeta_ref[:, p * PAIR:(p + 1) * PAIR]).astype(BF16)
        kv[p, lo] = _dot_tn(k_dec, blk(v_ref, p, lo)) * same
    intra = {}
    for p, lo in units:
        s0, s1 = scores[p, lo]
        vp = blk(v_ref, p, lo)
        intra[p, lo] = jnp.where(left,
                                 _dot((s0 * intra_ref[2 * p]).astype(BF16), vp),
                                 _dot((s1 * intra_ref[2 * p + 1]).astype(BF16), vp))
    outs = {}
    for p in range(n_pairs):
        state = state_ref[p]
        for lo in range(0, q_ref.shape[1], c):
            q_dec = (blk(q_ref, p, lo).astype(F32) * xi_ref[:, p * PAIR:(p + 1) * PAIR]).astype(BF16)
            outs[p, lo] = intra[p, lo] + _dot(q_dec, state.astype(BF16))
            state = sdec_ref[p] * state + kv[p, lo]
        state_ref[p] = state
    mus = {u: _dot(outs[u].astype(BF16), head_mean) for u in units}
    xcs = {u: outs[u] - mus[u] for u in units}
    variances = {u: _dot((xcs[u] * xcs[u]).astype(BF16), head_mean) for u in units}
    for p, lo in units:
        y = xcs[p, lo] * lax.rsqrt(variances[p, lo] + NORM_EPS)
        g = blk(g_ref, p, lo).astype(F32)
        o_ref[0, lo:lo + c, p * PAIR:(p + 1) * PAIR] = (g * _sigmoid(g) * y).astype(BF16)


def _retention(proj3d, width):
    batch, seq, _ = proj3d.shape
    n_heads = width // HEAD_DIM
    tc = min(RET_STEP, seq)
    intra, zeta, xi, sdec, same = _retention_tables(n_heads)
    col_spec = lambda i: pl.BlockSpec((1, tc, width), lambda b, j: (b, j, i))
    return pl.pallas_call(
        _retention_kernel,
        grid=(batch, seq // tc),
        in_specs=[col_spec(0), col_spec(1), col_spec(2), col_spec(3),
                  _resident(intra.shape), _resident(zeta.shape), _resident(xi.shape),
                  _resident(sdec.shape), _resident(same.shape)],
        out_specs=pl.BlockSpec((1, tc, width), lambda b, j: (b, j, 0)),
        out_shape=jax.ShapeDtypeStruct((batch, seq, width), BF16),
        scratch_shapes=[pltpu.VMEM((n_heads // 2, PAIR, PAIR), F32)],
        compiler_params=_params("parallel", "arbitrary"),
        name="retention",
    )(proj3d, proj3d, proj3d, proj3d, intra, zeta, xi, sdec, same)


def _s5_tables(lam_re, lam_im, b_re, b_im, c_re, c_im, log_step):
    n_groups, n_state = lam_re.shape
    n_chan = b_re.shape[-1]
    L = S5_CHUNK
    lr, li = lam_re.astype(F32), lam_im.astype(F32)
    delta = jnp.exp(log_step.astype(F32))[:, None]
    steps = jnp.arange(L + 1, dtype=F32)[:, None, None]
    mag = jnp.exp(steps * (lr * delta))
    pow_re = mag * jnp.cos(steps * (li * delta))
    pow_im = mag * jnp.sin(steps * (li * delta))
    bar_re, bar_im = pow_re[1], pow_im[1]
    den = lr * lr + li * li
    coef_re = ((bar_re - 1.0) * lr + bar_im * li) / den
    coef_im = (bar_im * lr - (bar_re - 1.0) * li) / den
    br, bi = b_re.astype(F32), b_im.astype(F32)
    bbar_re = coef_re[:, :, None] * br - coef_im[:, :, None] * bi
    bbar_im = coef_re[:, :, None] * bi + coef_im[:, :, None] * br
    gpk = lambda t: jnp.transpose(t, (1, 2, 0))
    crt, cit = jnp.swapaxes(c_re.astype(F32), 1, 2), jnp.swapaxes(c_im.astype(F32), 1, 2)
    pr, pi = gpk(pow_re)[..., None], gpk(pow_im)[..., None]
    cp_re = crt[:, :, None, :] * pr - cit[:, :, None, :] * pi
    cp_im = crt[:, :, None, :] * pi + cit[:, :, None, :] * pr
    kern = (jnp.einsum('gpkc,gpd->gdkc', cp_re, bbar_re)
            - jnp.einsum('gpkc,gpd->gdkc', cp_im, bbar_im))
    padded = jnp.concatenate([jnp.zeros_like(kern[:, :, :L]), kern[:, :, :L]], axis=2)
    toe = jnp.stack([padded[:, :, L - s:2 * L - s] for s in range(L)], axis=1)
    toe = toe.reshape(n_groups, L * n_chan, L * n_chan)
    dec = lambda t: jnp.transpose(t[L - 1 - jnp.arange(L)], (1, 0, 2))[:, :, None, :]
    brt, bit = jnp.swapaxes(bbar_re, 1, 2)[:, None], jnp.swapaxes(bbar_im, 1, 2)[:, None]
    e_re = (dec(pow_re) * brt - dec(pow_im) * bit).reshape(n_groups, L * n_chan, n_state)
    e_im = (dec(pow_re) * bit + dec(pow_im) * brt).reshape(n_groups, L * n_chan, n_state)
    b_pow = jnp.concatenate([e_re, e_im, e_im, e_re], axis=-1)
    from_state = lambda m: m[:, :, 1:].reshape(n_groups, n_state, L * n_chan)
    c_pow = jnp.concatenate([from_state(cp_re), -from_state(cp_im)], axis=1)
    a1 = jnp.concatenate([pow_re[L], pow_re[L]], axis=-1).reshape(1, n_groups * 2 * n_state)
    a2 = jnp.concatenate([-pow_im[L], pow_im[L]], axis=-1).reshape(1, n_groups * 2 * n_state)
    return toe.astype(BF16), b_pow.astype(BF16), c_pow.astype(BF16), a1, a2


def _granule_transpose(x):
    n = len(x)
    half = n // 2
    lane = lax.broadcasted_iota(jnp.int32, (1, LANES), 1)

    def rotate(cols, g):
        c0, c1 = cols
        if g >= half:
            c0, c1, g = c1, c0, g - half
        if g == 0:
            return [c0, c1]
        s = g * S5_GROUP
        r0, r1 = pltpu.roll(c0, s, 1), pltpu.roll(c1, s, 1)
        wrapped = lane < s
        return [jnp.where(wrapped, r1, r0), jnp.where(wrapped, r0, r1)]

    w = [rotate(x[i], i) for i in range(n)]
    for bit in range(3):
        d = 1 << bit
        take = ((lane // S5_GROUP) >> bit) & 1 == 1
        w = [[jnp.where(take, w[(j + d) % n][col], w[j][col]) for col in range(2)]
             for j in range(n)]
    w = [[w[j][0], w[(j + half) % n][1]] for j in range(n)]
    return [rotate(w[(-j) % n], (-j) % n) for j in range(n)]


def _s5_kernel(u_ref, toe_ref, bpow_ref, cpow_ref, a1_ref, a2_ref, d_ref, wglu_ref, o_ref,
               carry_ref, *, batch):
    n_chunks = u_ref.shape[0]
    rows = n_chunks * batch
    width = u_ref.shape[2]
    st = 2 * S5_STATE
    fold = S5_CHUNK * S5_GROUP

    @pl.when(pl.program_id(0) == 0)
    def _():
        carry_ref[...] = jnp.zeros(carry_ref.shape, F32)

    def slab(t):
        return u_ref[:, t * batch:(t + 1) * batch, :].reshape(rows, width)

    n_half = fold // S5_GROUP
    y_cols = [[None] * (width // LANES) for _ in range(S5_CHUNK)]
    for half in range(width // fold):
        base = half * fold
        x = _granule_transpose(
            [[slab(t)[:, base + c * LANES: base + (c + 1) * LANES] for c in range(2)]
             for t in range(S5_CHUNK)])
        y_half = []
        for j0 in range(0, n_half, S5_GROUP_BATCH):
            groups = range(half * n_half + j0, half * n_half + j0 + S5_GROUP_BATCH)
            folded = {g: jnp.concatenate(x[g % n_half], axis=1).astype(BF16) for g in groups}
            x_end = {g: _dot(folded[g], bpow_ref[g]) for g in groups}
            x_start = {}
            for g in groups:
                lanes = slice(g * st, (g + 1) * st)
                a1, a2 = a1_ref[:, lanes], a2_ref[:, lanes]
                xs, xw = carry_ref[0, :, lanes], carry_ref[1, :, lanes]
                starts = []
                for ch in range(n_chunks):
                    starts.append(xs)
                    e = x_end[g][ch * batch:(ch + 1) * batch]
                    xs, xw = a1 * xs + a2 * xw + e[:, :st], a1 * xw - a2 * xs + e[:, st:]
                carry_ref[0, :, lanes] = xs
                carry_ref[1, :, lanes] = xw
                x_start[g] = jnp.concatenate(starts, axis=0).astype(BF16)
            for g in groups:
                y = _dot(folded[g], toe_ref[g]) + _dot(x_start[g], cpow_ref[g])
                y_half.append([y[:, :LANES], y[:, LANES:]])
        y_t = _granule_transpose(y_half)
        for t in range(S5_CHUNK):
            y_cols[t][2 * half], y_cols[t][2 * half + 1] = y_t[t]

    for t in range(0, S5_CHUNK, 2):
        outs = []
        for tt in (t, t + 1):
            y = jnp.concatenate(y_cols[tt], axis=1) + d_ref[...] * slab(tt)
            y = jax.nn.gelu(y)
            z = y * _sigmoid(_dot(y.astype(BF16), wglu_ref[...]))
            outs.append(z.reshape(n_chunks, batch, width))
        o_ref[:, t * batch:(t + 2) * batch, :] = jnp.concatenate(outs, axis=1).astype(BF16)


def _s5(u_tm, batch, toe, b_pow, c_pow, a1, a2, d_skip, w_glu):
    seq, _, width = u_tm.shape
    span = S5_CHUNK * batch
    n_chunks = min(S5_STEP_CHUNKS, seq // S5_CHUNK)
    total = seq // S5_CHUNK
    u3 = u_tm.reshape(total, span, width)
    out = pl.pallas_call(
        functools.partial(_s5_kernel, batch=batch),
        grid=(total // n_chunks,),
        in_specs=[pl.BlockSpec((n_chunks, span, width), lambda j: (j, 0, 0)),
                  _resident(toe.shape), _resident(b_pow.shape), _resident(c_pow.shape),
                  _resident(a1.shape), _resident(a2.shape),
                  _resident((1, width)), _resident(w_glu.shape)],
        out_specs=pl.BlockSpec((n_chunks, span, width), lambda j: (j, 0, 0)),
        out_shape=jax.ShapeDtypeStruct((total, span, width), BF16),
        scratch_shapes=[pltpu.VMEM((2, batch, a1.shape[1]), F32)],
        compiler_params=_params("arbitrary"),
        name="s5_chunked",
    )(u3, toe, b_pow, c_pow, a1, a2, d_skip.reshape(1, width).astype(F32), w_glu.astype(BF16))
    return out.reshape(seq * batch, width)


def _diff_attn_kernel(q_ref, k_ref, v_ref, lam_ref, subln_ref, o_ref,
                      vt_ref, qm_ref, sa_ref, sb_ref, m_ref, acc_ref, *, tile):
    tq, tk = tile
    ratio = tq // tk
    seq = k_ref.shape[1]
    n_tiles = seq // tq
    lane = lax.broadcasted_iota(jnp.int32, (tk, PAIR), 1)
    zero = jnp.zeros((tk, PAIR), BF16)
    for lo in range(0, seq, tk):
        vt_ref[0:PAIR, lo:lo + tk] = v_ref[0, lo:lo + tk, :].astype(F32).T.astype(BF16)
        q = q_ref[0, lo:lo + tk, :]
        qm_ref[0, lo:lo + tk, :] = jnp.where(lane < HEAD_DIM, q, zero)
        qm_ref[1, lo:lo + tk, :] = jnp.where(lane < HEAD_DIM, zero, q)
    vt_ref[PAIR:, :] = jnp.ones((vt_ref.shape[0] - PAIR, seq), BF16)

    lam_p = lam_ref[...]
    lam = (jnp.exp(jnp.sum(lam_p[0:1] * lam_p[1:2], axis=-1, keepdims=True))
           - jnp.exp(jnp.sum(lam_p[2:3] * lam_p[3:4], axis=-1, keepdims=True))
           + LAMBDA_INIT_ODD)
    out_gain = subln_ref[...] * (1.0 - LAMBDA_INIT_ODD)

    def scores(qi, kj, i, s_ref, d=0):
        k = k_ref[0, pl.ds(pl.multiple_of(kj * tk, tk), tk), :]
        q = qm_ref[i, pl.ds(pl.multiple_of(qi * tq + d * tk, tk), tq - d * tk), :]
        s_ref[:, d * tk:] = _dot_nt(k, q)

    def consume(kj, i, s_ref, d=None):
        cols = slice(0, tq) if d is None else slice(d * tk, tq)
        s = s_ref[:, cols]
        if d is not None:
            k_pos = lax.broadcasted_iota(jnp.int32, s.shape, 0)
            q_pos = lax.broadcasted_iota(jnp.int32, s.shape, 1)
            s = jnp.where(k_pos <= q_pos, s, -jnp.inf)
        m_prev = m_ref[i, :, cols]
        m_part = jnp.max(s.reshape(tk // 8, 8, s.shape[1]), axis=0)
        m_new = jnp.maximum(m_prev, jnp.max(m_part, axis=0, keepdims=True))
        alpha = jnp.exp2(m_prev - m_new)
        p = jnp.exp2((s - m_new).astype(BF16))
        v_t = vt_ref[:, pl.ds(pl.multiple_of(kj * tk, tk), tk)]
        acc_ref[i, :, cols] = alpha * acc_ref[i, :, cols] + _dot(v_t, p)
        m_ref[i, :, cols] = m_new

    scores(0, 0, 0, sa_ref)

    def tile_body(qi, carry):
        m_ref[...] = jnp.full(m_ref.shape, -jnp.inf, F32)
        acc_ref[...] = jnp.zeros(acc_ref.shape, F32)
        first = qi * ratio

        def full_body(kj, c):
            scores(qi, kj, 1, sb_ref)
            consume(kj, 0, sa_ref)
            scores(qi, kj + 1, 0, sa_ref)
            consume(kj, 1, sb_ref)
            return c

        lax.fori_loop(0, first, full_body, 0)
        for d in range(ratio):
            scores(qi, first + d, 1, sb_ref, d)
            consume(first + d, 0, sa_ref, d)
            if d + 1 < ratio:
                scores(qi, first + d + 1, 0, sa_ref, d + 1)
            else:
                scores(jnp.minimum(qi + 1, n_tiles - 1), 0, 0, sa_ref)
            consume(first + d, 1, sb_ref, d)

        acc0, acc1 = acc_ref[0], acc_ref[1]
        out = (acc0[:PAIR] * (1.0 / acc0[PAIR:PAIR + 1])
               - lam * (acc1[:PAIR] * (1.0 / acc1[PAIR:PAIR + 1])))
        ms = jnp.mean(out * out, axis=0, keepdims=True)
        out = out * lax.rsqrt(ms + NORM_EPS) * out_gain
        o_ref[0, pl.ds(pl.multiple_of(qi * tq, tq), tq), :] = out.T.astype(BF16)
        return carry

    lax.fori_loop(0, n_tiles, tile_body, 0)


def _diff_attention(qkv3d, n_heads, lam_params, subln):
    batch, seq, _ = qkv3d.shape
    tq, tk = min(ATTN_Q_TILE, seq), min(ATTN_K_BLOCK, seq)
    ones_rows = 16
    seq_spec = lambda off: pl.BlockSpec((1, seq, PAIR), lambda b, h: (b, 0, off + h))
    return pl.pallas_call(
        functools.partial(_diff_attn_kernel, tile=(tq, tk)),
        grid=(batch, n_heads),
        in_specs=[seq_spec(0), seq_spec(n_heads), seq_spec(2 * n_heads),
                  _resident(lam_params.shape), _resident((PAIR, 1))],
        out_specs=seq_spec(0),
        out_shape=jax.ShapeDtypeStruct((batch, seq, n_heads * PAIR), BF16),
        scratch_shapes=[pltpu.VMEM((PAIR + ones_rows, seq), BF16),
                        pltpu.VMEM((2, seq, PAIR), BF16),
                        pltpu.VMEM((tk, tq), F32), pltpu.VMEM((tk, tq), F32),
                        pltpu.VMEM((2, 1, tq), F32),
                        pltpu.VMEM((2, PAIR + ones_rows, tq), F32)],
        compiler_params=_params("parallel", "parallel"),
        name="diff_attention",
    )(qkv3d, qkv3d, qkv3d, lam_params, subln.reshape(PAIR, 1).astype(F32))


def _ffn_chunks(hidden):
    bounds = list(range(0, hidden, 1024)) + [hidden]
    return list(zip(bounds[:-1], bounds[1:]))


def _post_kernel(*refs, time_major, final):
    n_mix = len(time_major)
    h_ref = refs[0]
    mix_refs = refs[1:1 + n_mix]
    rest = list(refs[1 + n_mix:])
    perm_ref = rest.pop(1) if any(time_major) else None
    (p_ref, wo_ref, nf_ref, wg_ref, wu_ref, wd_ref, np_ref, wpg_ref, wpp_ref,
     fin_ref, o_ref) = rest
    rows = h_ref.shape[0] * h_ref.shape[1]
    flat = lambda ref: ref[...].reshape(rows, ref.shape[-1])
    h = flat(h_ref)
    row = 0
    for m_ref, is_time_major in zip(mix_refs, time_major):
        mix = flat(m_ref)
        if is_time_major:
            mix = _dot(perm_ref[...], mix).astype(BF16)
        width = mix.shape[1]
        h = h + _dot(mix, wo_ref[row:row + width, :])
        row += width
    hn = _rms(h, nf_ref[...]).astype(BF16)
    ffn = jnp.zeros_like(h)
    for lo, hi in _ffn_chunks(wg_ref.shape[1]):
        gate = _dot(hn, wg_ref[:, lo:hi])
        up = _dot(hn, wu_ref[:, lo:hi])
        ffn = ffn + _dot((gate * _sigmoid(gate) * up).astype(BF16), wd_ref[lo:hi, :])
    h = h + ffn
    hp = _rms(h, np_ref[...]).astype(BF16)
    ple_gate = _sigmoid(_dot(hp, wpg_ref[...]))
    h = h + _dot(flat(p_ref).astype(BF16), wpp_ref[...]) * ple_gate
    if final:
        h = _rms(h, fin_ref[...])
    o_ref[...] = h.reshape(o_ref.shape)


def _post_mixer(h3d, mixes, p4d, layer, w_out, norm_ffn, w_gate, w_up, w_down,
                norm_ple, w_ple_gate, w_ple_proj, final_norm, *, final):
    batch, seq, d_model = h3d.shape
    tt = min(ROW_TILE // batch, seq)
    rows = tt * batch
    bm_spec = lambda width: pl.BlockSpec((batch, tt, width), lambda j: (0, j, 0))
    tm_spec = lambda width: pl.BlockSpec((1, rows, width), lambda j: (j, 0, 0))
    p_spec = pl.BlockSpec((None, batch, tt, p4d.shape[3]), lambda j: (layer, 0, j, 0))
    vec = lambda v: v.reshape(v.shape[0], 1, d_model).astype(F32)
    weights = [w_out.astype(BF16), vec(norm_ffn), w_gate.astype(BF16), w_up.astype(BF16),
               w_down.astype(BF16), vec(norm_ple), w_ple_gate.astype(BF16),
               w_ple_proj.astype(BF16), vec(final_norm[None])]
    layer_of = [0, layer, layer, layer, layer, layer, layer, layer, 0]
    weight_specs = [pl.BlockSpec((None,) + w.shape[1:], lambda j, i=i: (i, 0, 0),
                                 pipeline_mode=pl.Buffered(1))
                    for w, i in zip(weights, layer_of)]
    time_major = tuple(tm for _, tm in mixes)
    mix_arrays = [m.reshape(seq // tt, rows, m.shape[-1]) if tm else m for m, tm in mixes]
    mix_specs = [(tm_spec if tm else bm_spec)(m.shape[-1]) for m, tm in mixes]
    extra, extra_specs = [], []
    if any(time_major):
        dst = jnp.arange(rows)
        src = (dst % tt) * batch + dst // tt
        extra = [(src[:, None] == jnp.arange(rows)[None, :]).astype(BF16)]
        extra_specs = [_resident((rows, rows))]
    return pl.pallas_call(
        functools.partial(_post_kernel, time_major=time_major, final=final),
        grid=(seq // tt,),
        in_specs=([bm_spec(d_model)] + mix_specs + [p_spec] + extra_specs
                  + weight_specs),
        out_specs=bm_spec(d_model),
        out_shape=jax.ShapeDtypeStruct((batch, seq, d_model), F32),
        compiler_params=_params("parallel"),
        name="outproj_ffn_ple",
    )(h3d, *mix_arrays, p4d, *extra, *weights)


def kernel(x, p, norm_mix, norm_ffn, norm_ple, ret_s5_w_in, ret_s5_w_out, s5_lambda_re, s5_lambda_im, s5_b_re, s5_b_im, s5_c_re, s5_c_im, s5_d, s5_log_step, s5_w_glu, diff_w_qkv, diff_w_o, diff_lambda_q1, diff_lambda_k1, diff_lambda_q2, diff_lambda_k2, diff_subln, ffn_w_gate, ffn_w_up, ffn_w_down, ple_w_proj, ple_w_gate, final_norm):
    batch, seq, d_model = x.shape
    tokens = batch * seq
    p4d = p.reshape(p.shape[0], batch, seq, p.shape[-1])

    s5_width = s5_w_glu.shape[-1]
    ret_width = ret_s5_w_in.shape[-1] - s5_width
    ret_width //= 4
    proj, u_tm = _norm_proj(x.reshape(tokens, d_model), norm_mix[0], ret_s5_w_in[0], batch, seq,
                            n_q=ret_width, n_k=ret_width, n_main=4 * ret_width,
                            q_scale=HEAD_DIM ** -0.5)
    ret = _retention(proj.reshape(batch, seq, 4 * ret_width), ret_width)
    toe, b_pow, c_pow, a1, a2 = _s5_tables(s5_lambda_re[0], s5_lambda_im[0], s5_b_re[0], s5_b_im[0],
                                           s5_c_re[0], s5_c_im[0], s5_log_step[0])
    ssm_tm = _s5(u_tm, batch, toe, b_pow, c_pow, a1, a2, s5_d[0], s5_w_glu[0])
    h = _post_mixer(x, [(ret, False), (ssm_tm, True)],
                    p4d, 0, ret_s5_w_out, norm_ffn, ffn_w_gate, ffn_w_up,
                    ffn_w_down, norm_ple, ple_w_gate, ple_w_proj, final_norm,
                    final=False)

    v_width = diff_w_o.shape[1]
    n_heads = v_width // PAIR
    (qkv,) = _norm_proj(h.reshape(tokens, d_model), norm_mix[1], diff_w_qkv[0], batch, seq,
                        n_q=v_width, n_k=v_width, n_main=3 * v_width,
                        q_scale=HEAD_DIM ** -0.5 * math.log2(math.e))
    lam_params = jnp.stack([diff_lambda_q1[0], diff_lambda_k1[0],
                            diff_lambda_q2[0], diff_lambda_k2[0]]).astype(F32)
    attn = _diff_attention(qkv.reshape(batch, seq, 3 * v_width), n_heads, lam_params, diff_subln[0])
    return _post_mixer(h, [(attn, False)],
                       p4d, 1, diff_w_o, norm_ffn, ffn_w_gate, ffn_w_up,
                       ffn_w_down, norm_ple, ple_w_gate, ple_w_proj, final_norm,
                       final=True)
```

```python
import functools
import math

import jax
import jax.numpy as jnp
from jax import lax
from jax.experimental import pallas as pl
from jax.experimental.pallas import tpu as pltpu

F32 = jnp.float32
BF16 = jnp.bfloat16

NORM_EPS = 1e-6
ROPE_THETA = 10000.0
HEAD_DIM = 64
PAIR = 2 * HEAD_DIM
RET_CHUNK = 128
S5_GROUP = 16
S5_STATE = 64
S5_CHUNK = 16
LANES = 128
LAMBDA_INIT_ODD = 0.8 - 0.6 * math.exp(-0.3 * 1)

V7X_VMEM_LIMIT_BYTES = 56 * 1024 * 1024
ROW_TILE = 512
PROJ_COL_CHUNK = 512
RET_STEP = 512
S5_STEP_CHUNKS = 32
S5_GROUP_BATCH = 8
ATTN_Q_TILE = 1024
ATTN_K_BLOCK = 512


def _params(*semantics):
    return pltpu.CompilerParams(dimension_semantics=semantics,
                                vmem_limit_bytes=V7X_VMEM_LIMIT_BYTES)


def _resident(shape):
    zeros = (0,) * len(shape)
    return pl.BlockSpec(shape, lambda *_: zeros, pipeline_mode=pl.Buffered(1))


def _rms(x, gain):
    return x * lax.rsqrt(jnp.mean(x * x, axis=-1, keepdims=True) + NORM_EPS) * gain


def _sigmoid(x):
    return 1.0 / (1.0 + jnp.exp(-x))


def _dot(a, b):
    return jnp.dot(a, b, preferred_element_type=F32)


def _dot_nt(a, b):
    return lax.dot_general(a, b, (((1,), (1,)), ((), ())), preferred_element_type=F32)


def _dot_tn(a, b):
    return lax.dot_general(a, b, (((0,), (0,)), ((), ())), preferred_element_type=F32)


def _rope_tables(seq, scale):
    inv = ROPE_THETA ** (-jnp.arange(0, HEAD_DIM, 2, dtype=F32) / HEAD_DIM)
    ang = jnp.arange(seq, dtype=F32)[:, None] * inv[None, :]
    reps = PAIR // (HEAD_DIM // 2)
    cos = jnp.tile(jnp.cos(ang), (1, reps))
    sin = jnp.tile(jnp.sin(ang), (1, reps))
    first_half = (jnp.arange(PAIR) % HEAD_DIM) < HEAD_DIM // 2
    sin = jnp.where(first_half[None, :], -sin, sin)
    return cos * scale, sin * scale


def _norm_proj_kernel(x_ref, g_ref, w_ref, cq_ref, sq_ref, ck_ref, sk_ref, *out_refs,
                      n_q, n_k, n_main):
    hn = _rms(x_ref[...], g_ref[...]).astype(BF16)
    chunk = PROJ_COL_CHUNK
    lane = lax.broadcasted_iota(jnp.int32, (1, chunk), 1)
    first_half = (lane % HEAD_DIM) < HEAD_DIM // 2
    reps = chunk // PAIR
    for lo in range(0, w_ref.shape[1], chunk):
        acc = _dot(hn, w_ref[:, lo:lo + chunk])
        if lo < n_q + n_k:
            cos_ref, sin_ref = (cq_ref, sq_ref) if lo < n_q else (ck_ref, sk_ref)
            cos = jnp.concatenate([cos_ref[...]] * reps, axis=1)
            sin = jnp.concatenate([sin_ref[...]] * reps, axis=1)
            rot = jnp.where(first_half,
                            pltpu.roll(acc, chunk - HEAD_DIM // 2, 1),
                            pltpu.roll(acc, HEAD_DIM // 2, 1))
            acc = acc * cos + rot * sin
        if lo < n_main:
            out_refs[0][:, lo:lo + chunk] = acc.astype(BF16)
        else:
            out_refs[1][:, pl.program_id(1), lo - n_main:lo - n_main + chunk] = acc


def _norm_proj(x2d, gain, w, batch, seq, *, n_q, n_k, n_main, q_scale):
    tokens, d_model = x2d.shape
    n_total = w.shape[1]
    n_rest = n_total - n_main
    tm = min(ROW_TILE, seq)
    nt = seq // tm
    cq, sq = _rope_tables(seq, q_scale)
    ck, sk = _rope_tables(seq, 1.0)
    table_spec = pl.BlockSpec((tm, PAIR), lambda j, b: (j, 0))
    out_shape = [jax.ShapeDtypeStruct((tokens, n_main), BF16)]
    out_specs = [pl.BlockSpec((tm, n_main), lambda j, b: (b * nt + j, 0))]
    if n_rest:
        out_shape.append(jax.ShapeDtypeStruct((seq, batch, n_rest), F32))
        out_specs.append(pl.BlockSpec((tm, batch, n_rest), lambda j, b: (j, 0, 0)))
    return pl.pallas_call(
        functools.partial(_norm_proj_kernel, n_q=n_q, n_k=n_k, n_main=n_main),
        grid=(nt, batch),
        in_specs=[
            pl.BlockSpec((tm, d_model), lambda j, b: (b * nt + j, 0)),
            _resident((1, d_model)),
            _resident((d_model, n_total)),
            table_spec, table_spec, table_spec, table_spec,
        ],
        out_specs=out_specs,
        out_shape=out_shape,
        compiler_params=_params("parallel", "arbitrary"),
        name="norm_proj_rope",
    )(x2d, gain.reshape(1, d_model), w.astype(BF16), cq, sq, ck, sk)


def _retention_tables(n_heads):
    c = RET_CHUNK
    gamma = 1.0 - 2.0 ** (-5.0 - jnp.arange(n_heads, dtype=F32))
    log_g = jnp.log(gamma)
    idx = jnp.arange(c, dtype=F32)
    rel = idx[:, None] - idx[None, :]
    intra = jnp.where(rel >= 0, jnp.exp(log_g[:, None, None] * jnp.maximum(rel, 0.0)), 0.0)
    zeta = jnp.exp(log_g[:, None] * (c - 1 - idx))
    xi = jnp.exp(log_g[:, None] * (idx + 1.0))
    chunk_decay = jnp.exp(log_g * c)
    per_lane = lambda t: jnp.repeat(t.T, HEAD_DIM, axis=1)
    head_of = jnp.arange(PAIR) // HEAD_DIM
    same_head = (head_of[:, None] == head_of[None, :]).astype(F32)
    pair_decay = chunk_decay.reshape(n_heads // 2, 2)[:, head_of]
    state_decay = pair_decay[:, :, None] * same_head[None]
    return intra, per_lane(zeta), per_lane(xi), state_decay, same_head


def _retention_kernel(q_ref, k_ref, v_ref, g_ref, intra_ref, zeta_ref, xi_ref,
                      sdec_ref, same_ref, o_ref, state_ref):
    n_pairs = state_ref.shape[0]
    c = RET_CHUNK

    @pl.when(pl.program_id(1) == 0)
    def _():
        state_ref[...] = jnp.zeros(state_ref.shape, F32)

    lane = lax.broadcasted_iota(jnp.int32, (c, PAIR), 1)
    left = lane < HEAD_DIM
    same = same_ref[...]
    head_mean = (same * (1.0 / HEAD_DIM)).astype(BF16)

    units = [(p, lo) for p in range(n_pairs) for lo in range(0, q_ref.shape[1], c)]
    blk = lambda ref, p, lo: ref[0, lo:lo + c, p * PAIR:(p + 1) * PAIR]
    zero = jnp.zeros((c, PAIR), BF16)
    scores = {}
    for p, lo in units:
        qp, kp = blk(q_ref, p, lo), blk(k_ref, p, lo)
        scores[p, lo] = (_dot_nt(jnp.where(left, qp, zero), kp),
                         _dot_nt(jnp.where(left, zero, qp), kp))
    kv = {}
    for p, lo in units:
        k_dec = (blk(k_ref, p, lo).astype(F32) * zeta_ref[:, p * PAIR:(p + 1) * PAIR]).astype(BF16)
        kv[p, lo] = _dot_tn(k_dec, blk(v_ref, p, lo)) * same
    intra = {}
    for p, lo in units:
        s0, s1 = scores[p, lo]
        vp = blk(v_ref, p, lo)
        intra[p, lo] = jnp.where(left,
                                 _dot((s0 * intra_ref[2 * p]).astype(BF16), vp),
                                 _dot((s1 * intra_ref[2 * p + 1]).astype(BF16), vp))
    outs = {}
    for p in range(n_pairs):
        state = state_ref[p]
        for lo in range(0, q_ref.shape[1], c):
            q_dec = (blk(q_ref, p, lo).astype(F32) * xi_ref[:, p * PAIR:(p + 1) * PAIR]).astype(BF16)
            outs[p, lo] = intra[p, lo] + _dot(q_dec, state.astype(BF16))
            state = sdec_ref[p] * state + kv[p, lo]
        state_ref[p] = state
    mus = {u: _dot(outs[u].astype(BF16), head_mean) for u in units}
    xcs = {u: outs[u] - mus[u] for u in units}
    variances = {u: _dot((xcs[u] * xcs[u]).astype(BF16), head_mean) for u in units}
    for p, lo in units:
        y = xcs[p, lo] * lax.rsqrt(variances[p, lo] + NORM_EPS)
        g = blk(g_ref, p, lo).astype(F32)
        o_ref[0, lo:lo + c, p * PAIR:(p + 1) * PAIR] = (g * _sigmoid(g) * y).astype(BF16)


def _retention(proj3d, width):
    batch, seq, _ = proj3d.shape
    n_heads = width // HEAD_DIM
    tc = min(RET_STEP, seq)
    intra, zeta, xi, sdec, same = _retention_tables(n_heads)
    col_spec = lambda i: pl.BlockSpec((1, tc, width), lambda b, j: (b, j, i))
    return pl.pallas_call(
        _retention_kernel,
        grid=(batch, seq // tc),
        in_specs=[col_spec(0), col_spec(1), col_spec(2), col_spec(3),
                  _resident(intra.shape), _resident(zeta.shape), _resident(xi.shape),
                  _resident(sdec.shape), _resident(same.shape)],
        out_specs=pl.BlockSpec((1, tc, width), lambda b, j: (b, j, 0)),
        out_shape=jax.ShapeDtypeStruct((batch, seq, width), BF16),
        scratch_shapes=[pltpu.VMEM((n_heads // 2, PAIR, PAIR), F32)],
        compiler_params=_params("parallel", "arbitrary"),
        name="retention",
    )(proj3d, proj3d, proj3d, proj3d, intra, zeta, xi, sdec, same)


def _s5_tables(lam_re, lam_im, b_re, b_im, c_re, c_im, log_step):
    n_groups, n_state = lam_re.shape
    n_chan = b_re.shape[-1]
    L = S5_CHUNK
    lr, li = lam_re.astype(F32), lam_im.astype(F32)
    delta = jnp.exp(log_step.astype(F32))[:, None]
    steps = jnp.arange(L + 1, dtype=F32)[:, None, None]
    mag = jnp.exp(steps * (lr * delta))
    pow_re = mag * jnp.cos(steps * (li * delta))
    pow_im = mag * jnp.sin(steps * (li * delta))
    bar_re, bar_im = pow_re[1], pow_im[1]
    den = lr * lr + li * li
    coef_re = ((bar_re - 1.0) * lr + bar_im * li) / den
    coef_im = (bar_im * lr - (bar_re - 1.0) * li) / den
    br, bi = b_re.astype(F32), b_im.astype(F32)
    bbar_re = coef_re[:, :, None] * br - coef_im[:, :, None] * bi
    bbar_im = coef_re[:, :, None] * bi + coef_im[:, :, None] * br
    gpk = lambda t: jnp.transpose(t, (1, 2, 0))
    crt, cit = jnp.swapaxes(c_re.astype(F32), 1, 2), jnp.swapaxes(c_im.astype(F32), 1, 2)
    pr, pi = gpk(pow_re)[..., None], gpk(pow_im)[..., None]
    cp_re = crt[:, :, None, :] * pr - cit[:, :, None, :] * pi
    cp_im = crt[:, :, None, :] * pi + cit[:, :, None, :] * pr
    kern = (jnp.einsum('gpkc,gpd->gdkc', cp_re, bbar_re)
            - jnp.einsum('gpkc,gpd->gdkc', cp_im, bbar_im))
    flat = kern[:, :, :L].reshape(n_groups, n_chan, L * n_chan)
    toe = jnp.concatenate([jnp.pad(flat, ((0, 0), (0, 0), (s * n_chan, 0)))[:, :, :L * n_chan]
                           for s in range(L)], axis=1)
    quad = lambda a, b, c, d: jnp.concatenate([a, b, c, d], axis=-1)
    dec = lambda t: jnp.transpose(t[L - 1 - jnp.arange(L)], (1, 0, 2))[:, :, None, :]
    dr, di = dec(pow_re), dec(pow_im)
    brt, bit = jnp.swapaxes(bbar_re, 1, 2)[:, None], jnp.swapaxes(bbar_im, 1, 2)[:, None]
    b_pow = (quad(dr, dr, dr, dr) * quad(brt, bit, bit, brt)
             + quad(di, di, di, di) * quad(-bit, brt, brt, -bit))
    b_pow = b_pow.reshape(n_groups, L * n_chan, 4 * n_state)
    from_state = lambda m: m[:, :, 1:].reshape(n_groups, n_state, L * n_chan)
    c_pow = jnp.concatenate([from_state(cp_re), -from_state(cp_im)], axis=1)
    a1 = jnp.concatenate([pow_re[L], pow_re[L]], axis=-1).reshape(1, n_groups * 2 * n_state)
    a2 = jnp.concatenate([-pow_im[L], pow_im[L]], axis=-1).reshape(1, n_groups * 2 * n_state)
    return toe.astype(BF16), b_pow.astype(BF16), c_pow.astype(BF16), a1, a2


def _granule_transpose(x):
    n = len(x)
    half = n // 2
    lane = lax.broadcasted_iota(jnp.int32, (1, LANES), 1)

    def rotate(cols, g):
        c0, c1 = cols
        if g >= half:
            c0, c1, g = c1, c0, g - half
        if g == 0:
            return [c0, c1]
        s = g * S5_GROUP
        r0, r1 = pltpu.roll(c0, s, 1), pltpu.roll(c1, s, 1)
        wrapped = lane < s
        return [jnp.where(wrapped, r1, r0), jnp.where(wrapped, r0, r1)]

    w = [rotate(x[i], i) for i in range(n)]
    for bit in range(3):
        d = 1 << bit
        take = ((lane // S5_GROUP) >> bit) & 1 == 1
        w = [[jnp.where(take, w[(j + d) % n][col], w[j][col]) for col in range(2)]
             for j in range(n)]
    w = [[w[j][0], w[(j + half) % n][1]] for j in range(n)]
    return [rotate(w[(-j) % n], (-j) % n) for j in range(n)]


def _s5_kernel(u_ref, toe_ref, bpow_ref, cpow_ref, a1_ref, a2_ref, d_ref, wglu_ref, o_ref,
               carry_ref, *, batch):
    n_chunks = u_ref.shape[0]
    rows = n_chunks * batch
    width = u_ref.shape[2]
    st = 2 * S5_STATE
    fold = S5_CHUNK * S5_GROUP

    @pl.when(pl.program_id(0) == 0)
    def _():
        carry_ref[...] = jnp.zeros(carry_ref.shape, F32)

    def slab(t):
        return u_ref[:, t * batch:(t + 1) * batch, :].reshape(rows, width)

    n_half = fold // S5_GROUP
    y_cols = [[None] * (width // LANES) for _ in range(S5_CHUNK)]
    for half in range(width // fold):
        base = half * fold
        x = _granule_transpose(
            [[slab(t)[:, base + c * LANES: base + (c + 1) * LANES] for c in range(2)]
             for t in range(S5_CHUNK)])
        y_half = []
        for j0 in range(0, n_half, S5_GROUP_BATCH):
            groups = range(half * n_half + j0, half * n_half + j0 + S5_GROUP_BATCH)
            folded = {g: jnp.concatenate(x[g % n_half], axis=1).astype(BF16) for g in groups}
            x_end = {g: _dot(folded[g], bpow_ref[g]) for g in groups}
            x_start = {}
            for g in groups:
                lanes = slice(g * st, (g + 1) * st)
                a1, a2 = a1_ref[:, lanes], a2_ref[:, lanes]
                xs, xw = carry_ref[0, :, lanes], carry_ref[1, :, lanes]
                starts = []
                for ch in range(n_chunks):
                    starts.append(xs)
                    e = x_end[g][ch * batch:(ch + 1) * batch]
                    xs, xw = a1 * xs + a2 * xw + e[:, :st], a1 * xw - a2 * xs + e[:, st:]
                carry_ref[0, :, lanes] = xs
                carry_ref[1, :, lanes] = xw
                x_start[g] = jnp.concatenate(starts, axis=0).astype(BF16)
            for g in groups:
                y = _dot(folded[g], toe_ref[g]) + _dot(x_start[g], cpow_ref[g])
                y_half.append([y[:, :LANES], y[:, LANES:]])
        y_t = _granule_transpose(y_half)
        for t in range(S5_CHUNK):
            y_cols[t][2 * half], y_cols[t][2 * half + 1] = y_t[t]

    for t in range(0, S5_CHUNK, 2):
        outs = []
        for tt in (t, t + 1):
            y = jnp.concatenate(y_cols[tt], axis=1) + d_ref[...] * slab(tt)
            y = jax.nn.gelu(y)
            z = y * _sigmoid(_dot(y.astype(BF16), wglu_ref[...]))
            outs.append(z.reshape(n_chunks, batch, width))
        o_ref[:, t * batch:(t + 2) * batch, :] = jnp.concatenate(outs, axis=1).astype(BF16)


def _s5(u_tm, batch, toe, b_pow, c_pow, a1, a2, d_skip, w_glu):
    seq, _, width = u_tm.shape
    span = S5_CHUNK * batch
    n_chunks = min(S5_STEP_CHUNKS, seq // S5_CHUNK)
    total = seq // S5_CHUNK
    u3 = u_tm.reshape(total, span, width)
    out = pl.pallas_call(
        functools.partial(_s5_kernel, batch=batch),
        grid=(total // n_chunks,),
        in_specs=[pl.BlockSpec((n_chunks, span, width), lambda j: (j, 0, 0)),
                  _resident(toe.shape), _resident(b_pow.shape), _resident(c_pow.shape),
                  _resident(a1.shape), _resident(a2.shape),
                  _resident((1, width)), _resident(w_glu.shape)],
        out_specs=pl.BlockSpec((n_chunks, span, width), lambda j: (j, 0, 0)),
        out_shape=jax.ShapeDtypeStruct((total, span, width), BF16),
        scratch_shapes=[pltpu.VMEM((2, batch, a1.shape[1]), F32)],
        compiler_params=_params("arbitrary"),
        name="s5_chunked",
    )(u3, toe, b_pow, c_pow, a1, a2, d_skip.reshape(1, width).astype(F32), w_glu.astype(BF16))
    return out.reshape(seq * batch, width)


def _diff_attn_kernel(q_ref, k_ref, v_ref, lam_ref, subln_ref, o_ref,
                      vt_ref, qm_ref, sa_ref, sb_ref, m_ref, acc_ref, *, tile):
    tq, tk = tile
    ratio = tq // tk
    seq = k_ref.shape[1]
    n_tiles = seq // tq
    lane = lax.broadcasted_iota(jnp.int32, (tk, PAIR), 1)
    zero = jnp.zeros((tk, PAIR), BF16)
    for lo in range(0, seq, tk):
        vt_ref[0:PAIR, lo:lo + tk] = v_ref[0, lo:lo + tk, :].astype(F32).T.astype(BF16)
        q = q_ref[0, lo:lo + tk, :]
        qm_ref[0, lo:lo + tk, :] = jnp.where(lane < HEAD_DIM, q, zero)
        qm_ref[1, lo:lo + tk, :] = jnp.where(lane < HEAD_DIM, zero, q)
    vt_ref[PAIR:, :] = jnp.ones((vt_ref.shape[0] - PAIR, seq), BF16)

    lam_p = lam_ref[...]
    lam = (jnp.exp(jnp.sum(lam_p[0:1] * lam_p[1:2], axis=-1, keepdims=True))
           - jnp.exp(jnp.sum(lam_p[2:3] * lam_p[3:4], axis=-1, keepdims=True))
           + LAMBDA_INIT_ODD)
    out_gain = subln_ref[...] * (1.0 - LAMBDA_INIT_ODD)

    def scores(qi, kj, i, s_ref, d=0):
        k = k_ref[0, pl.ds(pl.multiple_of(kj * tk, tk), tk), :]
        q = qm_ref[i, pl.ds(pl.multiple_of(qi * tq + d * tk, tk), tq - d * tk), :]
        s_ref[:, d * tk:] = _dot_nt(k, q)

    def consume(kj, i, s_ref, d=None):
        cols = slice(0, tq) if d is None else slice(d * tk, tq)
        s = s_ref[:, cols]
        if d is not None:
            k_pos = lax.broadcasted_iota(jnp.int32, s.shape, 0)
            q_pos = lax.broadcasted_iota(jnp.int32, s.shape, 1)
            s = jnp.where(k_pos <= q_pos, s, -jnp.inf)
        m_prev = m_ref[i, :, cols]
        m_part = jnp.max(s.reshape(tk // 8, 8, s.shape[1]), axis=0)
        m_new = jnp.maximum(m_prev, jnp.max(m_part, axis=0, keepdims=True))
        alpha = jnp.exp2(m_prev - m_new)
        p = jnp.exp2((s - m_new).astype(BF16))
        v_t = vt_ref[:, pl.ds(pl.multiple_of(kj * tk, tk), tk)]
        acc_ref[i, :, cols] = alpha * acc_ref[i, :, cols] + _dot(v_t, p)
        m_ref[i, :, cols] = m_new

    scores(0, 0, 0, sa_ref)

    def tile_body(qi, carry):
        m_ref[...] = jnp.full(m_ref.shape, -jnp.inf, F32)
        acc_ref[...] = jnp.zeros(acc_ref.shape, F32)
        first = qi * ratio

        def full_body(kj, c):
            scores(qi, kj, 1, sb_ref)
            consume(kj, 0, sa_ref)
            scores(qi, kj + 1, 0, sa_ref)
            consume(kj, 1, sb_ref)
            return c

        lax.fori_loop(0, first, full_body, 0)
        for d in range(ratio):
            scores(qi, first + d, 1, sb_ref, d)
            consume(first + d, 0, sa_ref, d)
            if d + 1 < ratio:
                scores(qi, first + d + 1, 0, sa_ref, d + 1)
            else:
                scores(jnp.minimum(qi + 1, n_tiles - 1), 0, 0, sa_ref)
            consume(first + d, 1, sb_ref, d)

        acc0, acc1 = acc_ref[0], acc_ref[1]
        out = (acc0[:PAIR] * (1.0 / acc0[PAIR:PAIR + 1])
               - lam * (acc1[:PAIR] * (1.0 / acc1[PAIR:PAIR + 1])))
        ms = jnp.mean(out * out, axis=0, keepdims=True)
        out = out * lax.rsqrt(ms + NORM_EPS) * out_gain
        o_ref[0, pl.ds(pl.multiple_of(qi * tq, tq), tq), :] = out.T.astype(BF16)
        return carry

    lax.fori_loop(0, n_tiles, tile_body, 0)


def _diff_attention(qkv3d, n_heads, lam_params, subln):
    batch, seq, _ = qkv3d.shape
    tq, tk = min(ATTN_Q_TILE, seq), min(ATTN_K_BLOCK, seq)
    ones_rows = 16
    seq_spec = lambda off: pl.BlockSpec((1, seq, PAIR), lambda b, h: (b, 0, off + h))
    return pl.pallas_call(
        functools.partial(_diff_attn_kernel, tile=(tq, tk)),
        grid=(batch, n_heads),
        in_specs=[seq_spec(0), seq_spec(n_heads), seq_spec(2 * n_heads),
                  _resident(lam_params.shape), _resident((PAIR, 1))],
        out_specs=seq_spec(0),
        out_shape=jax.ShapeDtypeStruct((batch, seq, n_heads * PAIR), BF16),
        scratch_shapes=[pltpu.VMEM((PAIR + ones_rows, seq), BF16),
                        pltpu.VMEM((2, seq, PAIR), BF16),
                        pltpu.VMEM((tk, tq), F32), pltpu.VMEM((tk, tq), F32),
                        pltpu.VMEM((2, 1, tq), F32),
                        pltpu.VMEM((2, PAIR + ones_rows, tq), F32)],
        compiler_params=_params("parallel", "parallel"),
        name="diff_attention",
    )(qkv3d, qkv3d, qkv3d, lam_params, subln.reshape(PAIR, 1).astype(F32))


def _ffn_chunks(hidden):
    bounds = list(range(0, hidden, 1024)) + [hidden]
    return list(zip(bounds[:-1], bounds[1:]))


def _post_kernel(*refs, time_major, final):
    n_mix = len(time_major)
    h_ref = refs[0]
    mix_refs = refs[1:1 + n_mix]
    rest = list(refs[1 + n_mix:])
    perm_ref = rest.pop(1) if any(time_major) else None
    (p_ref, wo_ref, nf_ref, wg_ref, wu_ref, wd_ref, np_ref, wpg_ref, wpp_ref,
     fin_ref, o_ref) = rest
    rows = h_ref.shape[0] * h_ref.shape[1]
    flat = lambda ref: ref[...].reshape(rows, ref.shape[-1])
    h = flat(h_ref)
    row = 0
    for m_ref, is_time_major in zip(mix_refs, time_major):
        mix = flat(m_ref)
        if is_time_major:
            mix = _dot(perm_ref[...], mix).astype(BF16)
        width = mix.shape[1]
        h = h + _dot(mix, wo_ref[row:row + width, :])
        row += width
    hn = _rms(h, nf_ref[...]).astype(BF16)
    ffn = jnp.zeros_like(h)
    for lo, hi in _ffn_chunks(wg_ref.shape[1]):
        gate = _dot(hn, wg_ref[:, lo:hi])
        up = _dot(hn, wu_ref[:, lo:hi])
        ffn = ffn + _dot((gate * _sigmoid(gate) * up).astype(BF16), wd_ref[lo:hi, :])
    h = h + ffn
    hp = _rms(h, np_ref[...]).astype(BF16)
    ple_gate = _sigmoid(_dot(hp, wpg_ref[...]))
    h = h + _dot(flat(p_ref).astype(BF16), wpp_ref[...]) * ple_gate
    if final:
        h = _rms(h, fin_ref[...])
    o_ref[...] = h.reshape(o_ref.shape)


def _post_mixer(h3d, mixes, p4d, layer, w_out, norm_ffn, w_gate, w_up, w_down,
                norm_ple, w_ple_gate, w_ple_proj, final_norm, *, final):
    batch, seq, d_model = h3d.shape
    tt = min(ROW_TILE // batch, seq)
    rows = tt * batch
    bm_spec = lambda width: pl.BlockSpec((batch, tt, width), lambda j: (0, j, 0))
    tm_spec = lambda width: pl.BlockSpec((1, rows, width), lambda j: (j, 0, 0))
    p_spec = pl.BlockSpec((None, batch, tt, p4d.shape[3]), lambda j: (layer, 0, j, 0))
    vec = lambda v: v.reshape(v.shape[0], 1, d_model).astype(F32)
    weights = [w_out.astype(BF16), vec(norm_ffn), w_gate.astype(BF16), w_up.astype(BF16),
               w_down.astype(BF16), vec(norm_ple), w_ple_gate.astype(BF16),
               w_ple_proj.astype(BF16), vec(final_norm[None])]
    layer_of = [0, layer, layer, layer, layer, layer, layer, layer, 0]
    weight_specs = [pl.BlockSpec((None,) + w.shape[1:], lambda j, i=i: (i, 0, 0),
                                 pipeline_mode=pl.Buffered(1))
                    for w, i in zip(weights, layer_of)]
    time_major = tuple(tm for _, tm in mixes)
    mix_arrays = [m.reshape(seq // tt, rows, m.shape[-1]) if tm else m for m, tm in mixes]
    mix_specs = [(tm_spec if tm else bm_spec)(m.shape[-1]) for m, tm in mixes]
    extra, extra_specs = [], []
    if any(time_major):
        dst = jnp.arange(rows)
        src = (dst % tt) * batch + dst // tt
        extra = [(src[:, None] == jnp.arange(rows)[None, :]).astype(BF16)]
        extra_specs = [_resident((rows, rows))]
    return pl.pallas_call(
        functools.partial(_post_kernel, time_major=time_major, final=final),
        grid=(seq // tt,),
        in_specs=([bm_spec(d_model)] + mix_specs + [p_spec] + extra_specs
                  + weight_specs),
        out_specs=bm_spec(d_model),
        out_shape=jax.ShapeDtypeStruct((batch, seq, d_model), F32),
        compiler_params=_params("parallel"),
        name="outproj_ffn_ple",
    )(h3d, *mix_arrays, p4d, *extra, *weights)


def kernel(x, p, norm_mix, norm_ffn, norm_ple, ret_s5_w_in, ret_s5_w_out, s5_lambda_re, s5_lambda_im, s5_b_re, s5_b_im, s5_c_re, s5_c_im, s5_d, s5_log_step, s5_w_glu, diff_w_qkv, diff_w_o, diff_lambda_q1, diff_lambda_k1, diff_lambda_q2, diff_lambda_k2, diff_subln, ffn_w_gate, ffn_w_up, ffn_w_down, ple_w_proj, ple_w_gate, final_norm):
    batch, seq, d_model = x.shape
    tokens = batch * seq
    p4d = p.reshape(p.shape[0], batch, seq, p.shape[-1])

    s5_width = s5_w_glu.shape[-1]
    ret_width = ret_s5_w_in.shape[-1] - s5_width
    ret_width //= 4
    proj, u_tm = _norm_proj(x.reshape(tokens, d_model), norm_mix[0], ret_s5_w_in[0], batch, seq,
                            n_q=ret_width, n_k=ret_width, n_main=4 * ret_width,
                            q_scale=HEAD_DIM ** -0.5)
    ret = _retention(proj.reshape(batch, seq, 4 * ret_width), ret_width)
    toe, b_pow, c_pow, a1, a2 = _s5_tables(s5_lambda_re[0], s5_lambda_im[0], s5_b_re[0], s5_b_im[0],
                                           s5_c_re[0], s5_c_im[0], s5_log_step[0])
    ssm_tm = _s5(u_tm, batch, toe, b_pow, c_pow, a1, a2, s5_d[0], s5_w_glu[0])
    h = _post_mixer(x, [(ret, False), (ssm_tm, True)],
                    p4d, 0, ret_s5_w_out, norm_ffn, ffn_w_gate, ffn_w_up,
                    ffn_w_down, norm_ple, ple_w_gate, ple_w_proj, final_norm,
                    final=False)

    v_width = diff_w_o.shape[1]
    n_heads = v_width // PAIR
    (qkv,) = _norm_proj(h.reshape(tokens, d_model), norm_mix[1], diff_w_qkv[0], batch, seq,
                        n_q=v_width, n_k=v_width, n_main=3 * v_width,
                        q_scale=HEAD_DIM ** -0.5 * math.log2(math.e))
    lam_params = jnp.stack([diff_lambda_q1[0], diff_lambda_k1[0],
                            diff_lambda_q2[0], diff_lambda_k2[0]]).astype(F32)
    attn = _diff_attention(qkv.reshape(batch, seq, 3 * v_width), n_heads, lam_params, diff_subln[0])
    return _post_mixer(h, [(attn, False)],
                       p4d, 1, diff_w_o, norm_ffn, ffn_w_gate, ffn_w_up,
                       ffn_w_down, norm_ple, ple_w_gate, ple_w_proj, final_norm,
                       final=True)
```

```python
import functools
import math

import jax
import jax.numpy as jnp
from jax import lax
from jax.experimental import pallas as pl
from jax.experimental.pallas import tpu as pltpu

F32 = jnp.float32
BF16 = jnp.bfloat16

NORM_EPS = 1e-6
ROPE_THETA = 10000.0
HEAD_DIM = 64
PAIR = 2 * HEAD_DIM
RET_CHUNK = 128
S5_GROUP = 16
S5_STATE = 64
S5_CHUNK = 16
LANES = 128
LAMBDA_INIT_ODD = 0.8 - 0.6 * math.exp(-0.3 * 1)

V7X_VMEM_LIMIT_BYTES = 56 * 1024 * 1024
ROW_TILE = 512
PROJ_COL_CHUNK = 512
RET_STEP = 512
S5_STEP_CHUNKS = 32
S5_GROUP_BATCH = 8
ATTN_Q_TILE = 1024
ATTN_K_BLOCK = 512


def _params(*semantics):
    return pltpu.CompilerParams(dimension_semantics=semantics,
                                vmem_limit_bytes=V7X_VMEM_LIMIT_BYTES)


def _resident(shape):
    zeros = (0,) * len(shape)
    return pl.BlockSpec(shape, lambda *_: zeros, pipeline_mode=pl.Buffered(1))


def _rms(x, gain):
    return x * lax.rsqrt(jnp.mean(x * x, axis=-1, keepdims=True) + NORM_EPS) * gain


def _sigmoid(x):
    return 1.0 / (1.0 + jnp.exp(-x))


def _dot(a, b):
    return jnp.dot(a, b, preferred_element_type=F32)


def _dot_nt(a, b):
    return lax.dot_general(a, b, (((1,), (1,)), ((), ())), preferred_element_type=F32)


def _dot_tn(a, b):
    return lax.dot_general(a, b, (((0,), (0,)), ((), ())), preferred_element_type=F32)


def _rope_tables(seq, scale):
    inv = ROPE_THETA ** (-jnp.arange(0, HEAD_DIM, 2, dtype=F32) / HEAD_DIM)
    ang = jnp.arange(seq, dtype=F32)[:, None] * inv[None, :]
    reps = PAIR // (HEAD_DIM // 2)
    cos = jnp.tile(jnp.cos(ang), (1, reps))
    sin = jnp.tile(jnp.sin(ang), (1, reps))
    first_half = (jnp.arange(PAIR) % HEAD_DIM) < HEAD_DIM // 2
    sin = jnp.where(first_half[None, :], -sin, sin)
    return cos * scale, sin * scale


def _norm_proj_kernel(x_ref, g_ref, w_ref, cq_ref, sq_ref, ck_ref, sk_ref, *out_refs,
                      n_q, n_k, n_main):
    hn = _rms(x_ref[...], g_ref[...]).astype(BF16)
    chunk = PROJ_COL_CHUNK
    lane = lax.broadcasted_iota(jnp.int32, (1, chunk), 1)
    first_half = (lane % HEAD_DIM) < HEAD_DIM // 2
    reps = chunk // PAIR
    for lo in range(0, w_ref.shape[1], chunk):
        acc = _dot(hn, w_ref[:, lo:lo + chunk])
        if lo < n_q + n_k:
            cos_ref, sin_ref = (cq_ref, sq_ref) if lo < n_q else (ck_ref, sk_ref)
            cos = jnp.concatenate([cos_ref[...]] * reps, axis=1)
            sin = jnp.concatenate([sin_ref[...]] * reps, axis=1)
            rot = jnp.where(first_half,
                            pltpu.roll(acc, chunk - HEAD_DIM // 2, 1),
                            pltpu.roll(acc, HEAD_DIM // 2, 1))
            acc = acc * cos + rot * sin
        if lo < n_main:
            out_refs[0][:, lo:lo + chunk] = acc.astype(BF16)
        else:
            out_refs[1][:, pl.program_id(1), lo - n_main:lo - n_main + chunk] = acc


def _norm_proj(x2d, gain, w, batch, seq, *, n_q, n_k, n_main, q_scale):
    tokens, d_model = x2d.shape
    n_total = w.shape[1]
    n_rest = n_total - n_main
    tm = min(ROW_TILE, seq)
    nt = seq // tm
    cq, sq = _rope_tables(seq, q_scale)
    ck, sk = _rope_tables(seq, 1.0)
    table_spec = pl.BlockSpec((tm, PAIR), lambda j, b: (j, 0))
    out_shape = [jax.ShapeDtypeStruct((tokens, n_main), BF16)]
    out_specs = [pl.BlockSpec((tm, n_main), lambda j, b: (b * nt + j, 0))]
    if n_rest:
        out_shape.append(jax.ShapeDtypeStruct((seq, batch, n_rest), F32))
        out_specs.append(pl.BlockSpec((tm, batch, n_rest), lambda j, b: (j, 0, 0)))
    return pl.pallas_call(
        functools.partial(_norm_proj_kernel, n_q=n_q, n_k=n_k, n_main=n_main),
        grid=(nt, batch),
        in_specs=[
            pl.BlockSpec((tm, d_model), lambda j, b: (b * nt + j, 0)),
            _resident((1, d_model)),
            _resident((d_model, n_total)),
            table_spec, table_spec, table_spec, table_spec,
        ],
        out_specs=out_specs,
        out_shape=out_shape,
        compiler_params=_params("parallel", "arbitrary"),
        name="norm_proj_rope",
    )(x2d, gain.reshape(1, d_model), w.astype(BF16), cq, sq, ck, sk)


def _retention_tables(n_heads):
    c = RET_CHUNK
    gamma = 1.0 - 2.0 ** (-5.0 - jnp.arange(n_heads, dtype=F32))
    log_g = jnp.log(gamma)
    idx = jnp.arange(c, dtype=F32)
    rel = idx[:, None] - idx[None, :]
    intra = jnp.where(rel >= 0, jnp.exp(log_g[:, None, None] * jnp.maximum(rel, 0.0)), 0.0)
    zeta = jnp.exp(log_g[:, None] * (c - 1 - idx))
    xi = jnp.exp(log_g[:, None] * (idx + 1.0))
    chunk_decay = jnp.exp(log_g * c)
    per_lane = lambda t: jnp.repeat(t.T, HEAD_DIM, axis=1)
    head_of = jnp.arange(PAIR) // HEAD_DIM
    same_head = (head_of[:, None] == head_of[None, :]).astype(F32)
    pair_decay = chunk_decay.reshape(n_heads // 2, 2)[:, head_of]
    state_decay = pair_decay[:, :, None] * same_head[None]
    return intra, per_lane(zeta), per_lane(xi), state_decay, same_head


def _retention_kernel(q_ref, k_ref, v_ref, g_ref, intra_ref, zeta_ref, xi_ref,
                      sdec_ref, same_ref, o_ref, state_ref):
    n_pairs = state_ref.shape[0]
    c = RET_CHUNK

    @pl.when(pl.program_id(1) == 0)
    def _():
        state_ref[...] = jnp.zeros(state_ref.shape, F32)

    lane = lax.broadcasted_iota(jnp.int32, (c, PAIR), 1)
    left = lane < HEAD_DIM
    same = same_ref[...]
    head_mean = (same * (1.0 / HEAD_DIM)).astype(BF16)

    units = [(p, lo) for p in range(n_pairs) for lo in range(0, q_ref.shape[1], c)]
    blk = lambda ref, p, lo: ref[0, lo:lo + c, p * PAIR:(p + 1) * PAIR]
    zero = jnp.zeros((c, PAIR), BF16)
    scores = {}
    for p, lo in units:
        qp, kp = blk(q_ref, p, lo), blk(k_ref, p, lo)
        scores[p, lo] = (_dot_nt(jnp.where(left, qp, zero), kp),
                         _dot_nt(jnp.where(left, zero, qp), kp))
    kv = {}
    for p, lo in units:
        k_dec = (blk(k_ref, p, lo).astype(F32) * zeta_ref[:, p * PAIR:(p + 1) * PAIR]).astype(BF16)
        kv[p, lo] = _dot_tn(k_dec, blk(v_ref, p, lo)) * same
    intra = {}
    for p, lo in units:
        s0, s1 = scores[p, lo]
        vp = blk(v_ref, p, lo)
        intra[p, lo] = jnp.where(left,
                                 _dot((s0 * intra_ref[2 * p]).astype(BF16), vp),
                                 _dot((s1 * intra_ref[2 * p + 1]).astype(BF16), vp))
    outs = {}
    for p in range(n_pairs):
        state = state_ref[p]
        for lo in range(0, q_ref.shape[1], c):
            q_dec = (blk(q_ref, p, lo).astype(F32) * xi_ref[:, p * PAIR:(p + 1) * PAIR]).astype(BF16)
            outs[p, lo] = intra[p, lo] + _dot(q_dec, state.astype(BF16))
            state = sdec_ref[p] * state + kv[p, lo]
        state_ref[p] = state
    mus = {u: _dot(outs[u].astype(BF16), head_mean) for u in units}
    xcs = {u: outs[u] - mus[u] for u in units}
    variances = {u: _dot((xcs[u] * xcs[u]).astype(BF16), head_mean) for u in units}
    for p, lo in units:
        y = xcs[p, lo] * lax.rsqrt(variances[p, lo] + NORM_EPS)
        g = blk(g_ref, p, lo).astype(F32)
        o_ref[0, lo:lo + c, p * PAIR:(p + 1) * PAIR] = (g * _sigmoid(g) * y).astype(BF16)


def _retention(proj3d, width):
    batch, seq, _ = proj3d.shape
    n_heads = width // HEAD_DIM
    tc = min(RET_STEP, seq)
    intra, zeta, xi, sdec, same = _retention_tables(n_heads)
    col_spec = lambda i: pl.BlockSpec((1, tc, width), lambda b, j: (b, j, i))
    return pl.pallas_call(
        _retention_kernel,
        grid=(batch, seq // tc),
        in_specs=[col_spec(0), col_spec(1), col_spec(2), col_spec(3),
                  _resident(intra.shape), _resident(zeta.shape), _resident(xi.shape),
                  _resident(sdec.shape), _resident(same.shape)],
        out_specs=pl.BlockSpec((1, tc, width), lambda b, j: (b, j, 0)),
        out_shape=jax.ShapeDtypeStruct((batch, seq, width), BF16),
        scratch_shapes=[pltpu.VMEM((n_heads // 2, PAIR, PAIR), F32)],
        compiler_params=_params("parallel", "arbitrary"),
        name="retention",
    )(proj3d, proj3d, proj3d, proj3d, intra, zeta, xi, sdec, same)


def _s5_tables(lam_re, lam_im, b_re, b_im, c_re, c_im, log_step):
    n_groups, n_state = lam_re.shape
    n_chan = b_re.shape[-1]
    L = S5_CHUNK
    lr, li = lam_re.astype(F32), lam_im.astype(F32)
    delta = jnp.exp(log_step.astype(F32))[:, None]
    steps = jnp.arange(L + 1, dtype=F32)[:, None, None]
    mag = jnp.exp(steps * (lr * delta))
    pow_re = mag * jnp.cos(steps * (li * delta))
    pow_im = mag * jnp.sin(steps * (li * delta))
    bar_re, bar_im = pow_re[1], pow_im[1]
    den = lr * lr + li * li
    coef_re = ((bar_re - 1.0) * lr + bar_im * li) / den
    coef_im = (bar_im * lr - (bar_re - 1.0) * li) / den
    br, bi = b_re.astype(F32), b_im.astype(F32)
    bbar_re = coef_re[:, :, None] * br - coef_im[:, :, None] * bi
    bbar_im = coef_re[:, :, None] * bi + coef_im[:, :, None] * br
    gpk = lambda t: jnp.transpose(t, (1, 2, 0))
    crt, cit = jnp.swapaxes(c_re.astype(F32), 1, 2), jnp.swapaxes(c_im.astype(F32), 1, 2)
    pr, pi = gpk(pow_re)[..., None], gpk(pow_im)[..., None]
    cp_re = crt[:, :, None, :] * pr - cit[:, :, None, :] * pi
    cp_im = crt[:, :, None, :] * pi + cit[:, :, None, :] * pr
    kern = (jnp.einsum('gpkc,gpd->gdkc', cp_re, bbar_re)
            - jnp.einsum('gpkc,gpd->gdkc', cp_im, bbar_im))
    flat = kern[:, :, :L].reshape(n_groups, n_chan, L * n_chan)
    toe = jnp.concatenate([jnp.pad(flat, ((0, 0), (0, 0), (s * n_chan, 0)))[:, :, :L * n_chan]
                           for s in range(L)], axis=1)
    quad = lambda a, b, c, d: jnp.concatenate([a, b, c, d], axis=-1)
    dec = lambda t: jnp.transpose(t[L - 1 - jnp.arange(L)], (1, 0, 2))[:, :, None, :]
    dr, di = dec(pow_re), dec(pow_im)
    brt, bit = jnp.swapaxes(bbar_re, 1, 2)[:, None], jnp.swapaxes(bbar_im, 1, 2)[:, None]
    b_pow = (quad(dr, dr, dr, dr) * quad(brt, bit, bit, brt)
             + quad(di, di, di, di) * quad(-bit, brt, brt, -bit))
    b_pow = b_pow.reshape(n_groups, L * n_chan, 4 * n_state)
    from_state = lambda m: m[:, :, 1:].reshape(n_groups, n_state, L * n_chan)
    c_pow = jnp.concatenate([from_state(cp_re), -from_state(cp_im)], axis=1)
    a1 = jnp.concatenate([pow_re[L], pow_re[L]], axis=-1).reshape(1, n_groups * 2 * n_state)
    a2 = jnp.concatenate([-pow_im[L], pow_im[L]], axis=-1).reshape(1, n_groups * 2 * n_state)
    return toe.astype(BF16), b_pow.astype(BF16), c_pow.astype(BF16), a1, a2


def _granule_transpose(x):
    n = len(x)
    half = n // 2
    lane = lax.broadcasted_iota(jnp.int32, (1, LANES), 1)

    def rotate(cols, g):
        c0, c1 = cols
        if g >= half:
            c0, c1, g = c1, c0, g - half
        if g == 0:
            return [c0, c1]
        s = g * S5_GROUP
        r0, r1 = pltpu.roll(c0, s, 1), pltpu.roll(c1, s, 1)
        wrapped = lane < s
        return [jnp.where(wrapped, r1, r0), jnp.where(wrapped, r0, r1)]

    w = [rotate(x[i], i) for i in range(n)]
    for bit in range(3):
        d = 1 << bit
        take = ((lane // S5_GROUP) >> bit) & 1 == 1
        w = [[jnp.where(take, w[(j + d) % n][col], w[j][col]) for col in range(2)]
             for j in range(n)]
    w = [[w[j][0], w[(j + half) % n][1]] for j in range(n)]
    return [rotate(w[(-j) % n], (-j) % n) for j in range(n)]


def _s5_kernel(u_ref, toe_ref, bpow_ref, cpow_ref, a1_ref, a2_ref, d_ref, wglu_ref, o_ref,
               carry_ref, *, batch):
    n_chunks = u_ref.shape[0]
    rows = n_chunks * batch
    width = u_ref.shape[2]
    st = 2 * S5_STATE
    fold = S5_CHUNK * S5_GROUP

    @pl.when(pl.program_id(0) == 0)
    def _():
        carry_ref[...] = jnp.zeros(carry_ref.shape, F32)

    def slab(t):
        return u_ref[:, t * batch:(t + 1) * batch, :].reshape(rows, width)

    n_half = fold // S5_GROUP
    y_cols = [[None] * (width // LANES) for _ in range(S5_CHUNK)]
    for half in range(width // fold):
        base = half * fold
        x = _granule_transpose(
            [[slab(t)[:, base + c * LANES: base + (c + 1) * LANES] for c in range(2)]
             for t in range(S5_CHUNK)])
        y_half = []
        for j0 in range(0, n_half, S5_GROUP_BATCH):
            groups = range(half * n_half + j0, half * n_half + j0 + S5_GROUP_BATCH)
            folded = {g: jnp.concatenate(x[g % n_half], axis=1).astype(BF16) for g in groups}
            x_end = {g: _dot(folded[g], bpow_ref[g]) for g in groups}
            x_start = {}
            for g in groups:
                lanes = slice(g * st, (g + 1) * st)
                a1, a2 = a1_ref[:, lanes], a2_ref[:, lanes]
                xs, xw = carry_ref[0, :, lanes], carry_ref[1, :, lanes]
                starts = []
                for ch in range(n_chunks):
                    starts.append(xs)
                    e = x_end[g][ch * batch:(ch + 1) * batch]
                    xs, xw = a1 * xs + a2 * xw + e[:, :st], a1 * xw - a2 * xs + e[:, st:]
                carry_ref[0, :, lanes] = xs
                carry_ref[1, :, lanes] = xw
                x_start[g] = jnp.concatenate(starts, axis=0).astype(BF16)
            for g in groups:
                y = _dot(folded[g], toe_ref[g]) + _dot(x_start[g], cpow_ref[g])
                y_half.append([y[:, :LANES], y[:, LANES:]])
        y_t = _granule_transpose(y_half)
        for t in range(S5_CHUNK):
            y_cols[t][2 * half], y_cols[t][2 * half + 1] = y_t[t]

    for t in range(0, S5_CHUNK, 2):
        outs = []
        for tt in (t, t + 1):
            y = jnp.concatenate(y_cols[tt], axis=1) + d_ref[...] * slab(tt)
            y = jax.nn.gelu(y)
            z = y * _sigmoid(_dot(y.astype(BF16), wglu_ref[...]))
            outs.append(z.reshape(n_chunks, batch, width))
        o_ref[:, t * batch:(t + 2) * batch, :] = jnp.concatenate(outs, axis=1).astype(BF16)


def _s5(u_tm, batch, toe, b_pow, c_pow, a1, a2, d_skip, w_glu):
    seq, _, width = u_tm.shape
    span = S5_CHUNK * batch
    n_chunks = min(S5_STEP_CHUNKS, seq // S5_CHUNK)
    total = seq // S5_CHUNK
    u3 = u_tm.reshape(total, span, width)
    out = pl.pallas_call(
        functools.partial(_s5_kernel, batch=batch),
        grid=(total // n_chunks,),
        in_specs=[pl.BlockSpec((n_chunks, span, width), lambda j: (j, 0, 0)),
                  _resident(toe.shape), _resident(b_pow.shape), _resident(c_pow.shape),
                  _resident(a1.shape), _resident(a2.shape),
                  _resident((1, width)), _resident(w_glu.shape)],
        out_specs=pl.BlockSpec((n_chunks, span, width), lambda j: (j, 0, 0)),
        out_shape=jax.ShapeDtypeStruct((total, span, width), BF16),
        scratch_shapes=[pltpu.VMEM((2, batch, a1.shape[1]), F32)],
        compiler_params=_params("arbitrary"),
        name="s5_chunked",
    )(u3, toe, b_pow, c_pow, a1, a2, d_skip.reshape(1, width).astype(F32), w_glu.astype(BF16))
    return out.reshape(seq * batch, width)


def _diff_attn_kernel(q_ref, k_ref, v_ref, lam_ref, subln_ref, o_ref,
                      vt_ref, qm_ref, sa_ref, sb_ref, m_ref, acc_ref, *, tile):
    tq, tk = tile
    ratio = tq // tk
    seq = k_ref.shape[1]
    n_tiles = seq // tq
    lane = lax.broadcasted_iota(jnp.int32, (tk, PAIR), 1)
    zero = jnp.zeros((tk, PAIR), BF16)
    for lo in range(0, seq, tk):
        vt_ref[0:PAIR, lo:lo + tk] = v_ref[0, lo:lo + tk, :].astype(F32).T.astype(BF16)
        q = q_ref[0, lo:lo + tk, :]
        qm_ref[0, lo:lo + tk, :] = jnp.where(lane < HEAD_DIM, q, zero)
        qm_ref[1, lo:lo + tk, :] = jnp.where(lane < HEAD_DIM, zero, q)
    vt_ref[PAIR:, :] = jnp.ones((vt_ref.shape[0] - PAIR, seq), BF16)

    lam_p = lam_ref[...]
    lam = (jnp.exp(jnp.sum(lam_p[0:1] * lam_p[1:2], axis=-1, keepdims=True))
           - jnp.exp(jnp.sum(lam_p[2:3] * lam_p[3:4], axis=-1, keepdims=True))
           + LAMBDA_INIT_ODD)
    out_gain = subln_ref[...] * (1.0 - LAMBDA_INIT_ODD)

    def scores(qi, kj, i, s_ref, d=0):
        k = k_ref[0, pl.ds(pl.multiple_of(kj * tk, tk), tk), :]
        q = qm_ref[i, pl.ds(pl.multiple_of(qi * tq + d * tk, tk), tq - d * tk), :]
        s_ref[:, d * tk:] = _dot_nt(k, q).astype(BF16)

    def consume(kj, i, s_ref, d=None):
        cols = slice(0, tq) if d is None else slice(d * tk, tq)
        s = s_ref[:, cols]
        if d is not None:
            k_pos = lax.broadcasted_iota(jnp.int32, s.shape, 0)
            q_pos = lax.broadcasted_iota(jnp.int32, s.shape, 1)
            s = jnp.where(k_pos <= q_pos, s, jnp.asarray(-jnp.inf, BF16))
        m_prev = m_ref[i, :, cols]
        m_part = jnp.max(s.reshape(tk // 16, 16, s.shape[1]), axis=0)
        m_new = jnp.maximum(m_prev, jnp.max(m_part, axis=0, keepdims=True).astype(F32))
        alpha = jnp.exp2(m_prev - m_new)
        p = jnp.exp2(s - m_new.astype(BF16))
        v_t = vt_ref[:, pl.ds(pl.multiple_of(kj * tk, tk), tk)]
        acc_ref[i, :, cols] = alpha * acc_ref[i, :, cols] + _dot(v_t, p)
        m_ref[i, :, cols] = m_new

    scores(0, 0, 0, sa_ref)

    def tile_body(qi, carry):
        m_ref[...] = jnp.full(m_ref.shape, -jnp.inf, F32)
        acc_ref[...] = jnp.zeros(acc_ref.shape, F32)
        first = qi * ratio

        def full_body(kj, c):
            scores(qi, kj, 1, sb_ref)
            consume(kj, 0, sa_ref)
            scores(qi, kj + 1, 0, sa_ref)
            consume(kj, 1, sb_ref)
            return c

        lax.fori_loop(0, first, full_body, 0)
        for d in range(ratio):
            scores(qi, first + d, 1, sb_ref, d)
            consume(first + d, 0, sa_ref, d)
            if d + 1 < ratio:
                scores(qi, first + d + 1, 0, sa_ref, d + 1)
            else:
                scores(jnp.minimum(qi + 1, n_tiles - 1), 0, 0, sa_ref)
            consume(first + d, 1, sb_ref, d)

        acc0, acc1 = acc_ref[0], acc_ref[1]
        out = (acc0[:PAIR] * (1.0 / acc0[PAIR:PAIR + 1])
               - lam * (acc1[:PAIR] * (1.0 / acc1[PAIR:PAIR + 1])))
        ms = jnp.mean(out * out, axis=0, keepdims=True)
        out = out * lax.rsqrt(ms + NORM_EPS) * out_gain
        o_ref[0, pl.ds(pl.multiple_of(qi * tq, tq), tq), :] = out.T.astype(BF16)
        return carry

    lax.fori_loop(0, n_tiles, tile_body, 0)


def _diff_attention(qkv3d, n_heads, lam_params, subln):
    batch, seq, _ = qkv3d.shape
    tq, tk = min(ATTN_Q_TILE, seq), min(ATTN_K_BLOCK, seq)
    ones_rows = 16
    seq_spec = lambda off: pl.BlockSpec((1, seq, PAIR), lambda b, h: (b, 0, off + h))
    return pl.pallas_call(
        functools.partial(_diff_attn_kernel, tile=(tq, tk)),
        grid=(batch, n_heads),
        in_specs=[seq_spec(0), seq_spec(n_heads), seq_spec(2 * n_heads),
                  _resident(lam_params.shape), _resident((PAIR, 1))],
        out_specs=seq_spec(0),
        out_shape=jax.ShapeDtypeStruct((batch, seq, n_heads * PAIR), BF16),
        scratch_shapes=[pltpu.VMEM((PAIR + ones_rows, seq), BF16),
                        pltpu.VMEM((2, seq, PAIR), BF16),
                        pltpu.VMEM((tk, tq), BF16), pltpu.VMEM((tk, tq), BF16),
                        pltpu.VMEM((2, 1, tq), F32),
                        pltpu.VMEM((2, PAIR + ones_rows, tq), F32)],
        compiler_params=_params("parallel", "parallel"),
        name="diff_attention",
    )(qkv3d, qkv3d, qkv3d, lam_params, subln.reshape(PAIR, 1).astype(F32))


def _ffn_chunks(hidden):
    bounds = list(range(0, hidden, 1024)) + [hidden]
    return list(zip(bounds[:-1], bounds[1:]))


def _post_kernel(*refs, time_major, final):
    n_mix = len(time_major)
    h_ref = refs[0]
    mix_refs = refs[1:1 + n_mix]
    rest = list(refs[1 + n_mix:])
    perm_ref = rest.pop(1) if any(time_major) else None
    (p_ref, wo_ref, nf_ref, wg_ref, wu_ref, wd_ref, np_ref, wpg_ref, wpp_ref,
     fin_ref, o_ref) = rest
    rows = h_ref.shape[0] * h_ref.shape[1]
    flat = lambda ref: ref[...].reshape(rows, ref.shape[-1])
    h = flat(h_ref)
    row = 0
    for m_ref, is_time_major in zip(mix_refs, time_major):
        mix = flat(m_ref)
        if is_time_major:
            mix = _dot(perm_ref[...], mix).astype(BF16)
        width = mix.shape[1]
        h = h + _dot(mix, wo_ref[row:row + width, :])
        row += width
    hn = _rms(h, nf_ref[...]).astype(BF16)
    ffn = jnp.zeros_like(h)
    for lo, hi in _ffn_chunks(wg_ref.shape[1]):
        gate = _dot(hn, wg_ref[:, lo:hi])
        up = _dot(hn, wu_ref[:, lo:hi])
        ffn = ffn + _dot((gate * _sigmoid(gate) * up).astype(BF16), wd_ref[lo:hi, :])
    h = h + ffn
    hp = _rms(h, np_ref[...]).astype(BF16)
    ple_gate = _sigmoid(_dot(hp, wpg_ref[...]))
    h = h + _dot(flat(p_ref).astype(BF16), wpp_ref[...]) * ple_gate
    if final:
        h = _rms(h, fin_ref[...])
    o_ref[...] = h.reshape(o_ref.shape)


def _post_mixer(h3d, mixes, p4d, layer, w_out, norm_ffn, w_gate, w_up, w_down,
                norm_ple, w_ple_gate, w_ple_proj, final_norm, *, final):
    batch, seq, d_model = h3d.shape
    tt = min(ROW_TILE // batch, seq)
    rows = tt * batch
    bm_spec = lambda width: pl.BlockSpec((batch, tt, width), lambda j: (0, j, 0))
    tm_spec = lambda width: pl.BlockSpec((1, rows, width), lambda j: (j, 0, 0))
    p_spec = pl.BlockSpec((None, batch, tt, p4d.shape[3]), lambda j: (layer, 0, j, 0))
    vec = lambda v: v.reshape(v.shape[0], 1, d_model).astype(F32)
    weights = [w_out.astype(BF16), vec(norm_ffn), w_gate.astype(BF16), w_up.astype(BF16),
               w_down.astype(BF16), vec(norm_ple), w_ple_gate.astype(BF16),
               w_ple_proj.astype(BF16), vec(final_norm[None])]
    layer_of = [0, layer, layer, layer, layer, layer, layer, layer, 0]
    weight_specs = [pl.BlockSpec((None,) + w.shape[1:], lambda j, i=i: (i, 0, 0),
                                 pipeline_mode=pl.Buffered(1))
                    for w, i in zip(weights, layer_of)]
    time_major = tuple(tm for _, tm in mixes)
    mix_arrays = [m.reshape(seq // tt, rows, m.shape[-1]) if tm else m for m, tm in mixes]
    mix_specs = [(tm_spec if tm else bm_spec)(m.shape[-1]) for m, tm in mixes]
    extra, extra_specs = [], []
    if any(time_major):
        dst = jnp.arange(rows)
        src = (dst % tt) * batch + dst // tt
        extra = [(src[:, None] == jnp.arange(rows)[None, :]).astype(BF16)]
        extra_specs = [_resident((rows, rows))]
    return pl.pallas_call(
        functools.partial(_post_kernel, time_major=time_major, final=final),
        grid=(seq // tt,),
        in_specs=([bm_spec(d_model)] + mix_specs + [p_spec] + extra_specs
                  + weight_specs),
        out_specs=bm_spec(d_model),
        out_shape=jax.ShapeDtypeStruct((batch, seq, d_model), F32),
        compiler_params=_params("parallel"),
        name="outproj_ffn_ple",
    )(h3d, *mix_arrays, p4d, *extra, *weights)


def kernel(x, p, norm_mix, norm_ffn, norm_ple, ret_s5_w_in, ret_s5_w_out, s5_lambda_re, s5_lambda_im, s5_b_re, s5_b_im, s5_c_re, s5_c_im, s5_d, s5_log_step, s5_w_glu, diff_w_qkv, diff_w_o, diff_lambda_q1, diff_lambda_k1, diff_lambda_q2, diff_lambda_k2, diff_subln, ffn_w_gate, ffn_w_up, ffn_w_down, ple_w_proj, ple_w_gate, final_norm):
    batch, seq, d_model = x.shape
    tokens = batch * seq
    p4d = p.reshape(p.shape[0], batch, seq, p.shape[-1])

    s5_width = s5_w_glu.shape[-1]
    ret_width = ret_s5_w_in.shape[-1] - s5_width
    ret_width //= 4
    proj, u_tm = _norm_proj(x.reshape(tokens, d_model), norm_mix[0], ret_s5_w_in[0], batch, seq,
                            n_q=ret_width, n_k=ret_width, n_main=4 * ret_width,
                            q_scale=HEAD_DIM ** -0.5)
    ret = _retention(proj.reshape(batch, seq, 4 * ret_width), ret_width)
    toe, b_pow, c_pow, a1, a2 = _s5_tables(s5_lambda_re[0], s5_lambda_im[0], s5_b_re[0], s5_b_im[0],
                                           s5_c_re[0], s5_c_im[0], s5_log_step[0])
    ssm_tm = _s5(u_tm, batch, toe, b_pow, c_pow, a1, a2, s5_d[0], s5_w_glu[0])
    h = _post_mixer(x, [(ret, False), (ssm_tm, True)],
                    p4d, 0, ret_s5_w_out, norm_ffn, ffn_w_gate, ffn_w_up,
                    ffn_w_down, norm_ple, ple_w_gate, ple_w_proj, final_norm,
                    final=False)

    v_width = diff_w_o.shape[1]
    n_heads = v_width // PAIR
    (qkv,) = _norm_proj(h.reshape(tokens, d_model), norm_mix[1], diff_w_qkv[0], batch, seq,
                        n_q=v_width, n_k=v_width, n_main=3 * v_width,
                        q_scale=HEAD_DIM ** -0.5 * math.log2(math.e))
    lam_params = jnp.stack([diff_lambda_q1[0], diff_lambda_k1[0],
                            diff_lambda_q2[0], diff_lambda_k2[0]]).astype(F32)
    attn = _diff_attention(qkv.reshape(batch, seq, 3 * v_width), n_heads, lam_params, diff_subln[0])
    return _post_mixer(h, [(attn, False)],
                       p4d, 1, diff_w_o, norm_ffn, ffn_w_gate, ffn_w_up,
                       ffn_w_down, norm_ple, ple_w_gate, ple_w_proj, final_norm,
                       final=True)
```

```python
import functools
import math

import jax
import jax.numpy as jnp
from jax import lax
from jax.experimental import pallas as pl
from jax.experimental.pallas import tpu as pltpu

F32 = jnp.float32
BF16 = jnp.bfloat16

NORM_EPS = 1e-6
ROPE_THETA = 10000.0
HEAD_DIM = 64
PAIR = 2 * HEAD_DIM
RET_CHUNK = 128
S5_GROUP = 16
S5_STATE = 64
S5_CHUNK = 16
LANES = 128
LAMBDA_INIT_ODD = 0.8 - 0.6 * math.exp(-0.3 * 1)

V7X_VMEM_LIMIT_BYTES = 56 * 1024 * 1024
ROW_TILE = 512
PROJ_COL_CHUNK = 512
RET_STEP = 512
S5_STEP_CHUNKS = 32
S5_GROUP_BATCH = 8
ATTN_Q_TILE = 2048
ATTN_K_BLOCK = 512


def _params(*semantics):
    return pltpu.CompilerParams(dimension_semantics=semantics,
                                vmem_limit_bytes=V7X_VMEM_LIMIT_BYTES)


def _resident(shape):
    zeros = (0,) * len(shape)
    return pl.BlockSpec(shape, lambda *_: zeros, pipeline_mode=pl.Buffered(1))


def _rms(x, gain):
    return x * lax.rsqrt(jnp.mean(x * x, axis=-1, keepdims=True) + NORM_EPS) * gain


def _sigmoid(x):
    return 1.0 / (1.0 + jnp.exp(-x))


def _dot(a, b):
    return jnp.dot(a, b, preferred_element_type=F32)


def _dot_nt(a, b):
    return lax.dot_general(a, b, (((1,), (1,)), ((), ())), preferred_element_type=F32)


def _dot_tn(a, b):
    return lax.dot_general(a, b, (((0,), (0,)), ((), ())), preferred_element_type=F32)


def _rope_tables(seq, scale):
    inv = ROPE_THETA ** (-jnp.arange(0, HEAD_DIM, 2, dtype=F32) / HEAD_DIM)
    ang = jnp.arange(seq, dtype=F32)[:, None] * inv[None, :]
    reps = PAIR // (HEAD_DIM // 2)
    cos = jnp.tile(jnp.cos(ang), (1, reps))
    sin = jnp.tile(jnp.sin(ang), (1, reps))
    first_half = (jnp.arange(PAIR) % HEAD_DIM) < HEAD_DIM // 2
    sin = jnp.where(first_half[None, :], -sin, sin)
    return cos * scale, sin * scale


def _norm_proj_kernel(x_ref, g_ref, w_ref, cq_ref, sq_ref, ck_ref, sk_ref, *out_refs,
                      n_q, n_k, n_main):
    hn = _rms(x_ref[...], g_ref[...]).astype(BF16)
    chunk = PROJ_COL_CHUNK
    lane = lax.broadcasted_iota(jnp.int32, (1, chunk), 1)
    first_half = (lane % HEAD_DIM) < HEAD_DIM // 2
    reps = chunk // PAIR
    for lo in range(0, w_ref.shape[1], chunk):
        acc = _dot(hn, w_ref[:, lo:lo + chunk])
        if lo < n_q + n_k:
            cos_ref, sin_ref = (cq_ref, sq_ref) if lo < n_q else (ck_ref, sk_ref)
            cos = jnp.concatenate([cos_ref[...]] * reps, axis=1)
            sin = jnp.concatenate([sin_ref[...]] * reps, axis=1)
            rot = jnp.where(first_half,
                            pltpu.roll(acc, chunk - HEAD_DIM // 2, 1),
                            pltpu.roll(acc, HEAD_DIM // 2, 1))
            acc = acc * cos + rot * sin
        if lo < n_main:
            out_refs[0][:, lo:lo + chunk] = acc.astype(BF16)
        else:
            out_refs[1][:, pl.program_id(1), lo - n_main:lo - n_main + chunk] = acc


def _norm_proj(x2d, gain, w, batch, seq, *, n_q, n_k, n_main, q_scale):
    tokens, d_model = x2d.shape
    n_total = w.shape[1]
    n_rest = n_total - n_main
    tm = min(ROW_TILE, seq)
    nt = seq // tm
    cq, sq = _rope_tables(seq, q_scale)
    ck, sk = _rope_tables(seq, 1.0)
    table_spec = pl.BlockSpec((tm, PAIR), lambda j, b: (j, 0))
    out_shape = [jax.ShapeDtypeStruct((tokens, n_main), BF16)]
    out_specs = [pl.BlockSpec((tm, n_main), lambda j, b: (b * nt + j, 0))]
    if n_rest:
        out_shape.append(jax.ShapeDtypeStruct((seq, batch, n_rest), F32))
        out_specs.append(pl.BlockSpec((tm, batch, n_rest), lambda j, b: (j, 0, 0)))
    return pl.pallas_call(
        functools.partial(_norm_proj_kernel, n_q=n_q, n_k=n_k, n_main=n_main),
        grid=(nt, batch),
        in_specs=[
            pl.BlockSpec((tm, d_model), lambda j, b: (b * nt + j, 0)),
            _resident((1, d_model)),
            _resident((d_model, n_total)),
            table_spec, table_spec, table_spec, table_spec,
        ],
        out_specs=out_specs,
        out_shape=out_shape,
        compiler_params=_params("parallel", "arbitrary"),
        name="norm_proj_rope",
    )(x2d, gain.reshape(1, d_model), w.astype(BF16), cq, sq, ck, sk)


def _retention_tables(n_heads):
    c = RET_CHUNK
    gamma = 1.0 - 2.0 ** (-5.0 - jnp.arange(n_heads, dtype=F32))
    log_g = jnp.log(gamma)
    idx = jnp.arange(c, dtype=F32)
    rel = idx[:, None] - idx[None, :]
    intra = jnp.where(rel >= 0, jnp.exp(log_g[:, None, None] * jnp.maximum(rel, 0.0)), 0.0)
    zeta = jnp.exp(log_g[:, None] * (c - 1 - idx))
    xi = jnp.exp(log_g[:, None] * (idx + 1.0))
    chunk_decay = jnp.exp(log_g * c)
    per_lane = lambda t: jnp.repeat(t.T, HEAD_DIM, axis=1)
    head_of = jnp.arange(PAIR) // HEAD_DIM
    same_head = (head_of[:, None] == head_of[None, :]).astype(F32)
    pair_decay = chunk_decay.reshape(n_heads // 2, 2)[:, head_of]
    state_decay = pair_decay[:, :, None] * same_head[None]
    return intra, per_lane(zeta), per_lane(xi), state_decay, same_head


def _retention_kernel(q_ref, k_ref, v_ref, g_ref, intra_ref, zeta_ref, xi_ref,
                      sdec_ref, same_ref, o_ref, state_ref):
    n_pairs = state_ref.shape[0]
    c = RET_CHUNK

    @pl.when(pl.program_id(1) == 0)
    def _():
        state_ref[...] = jnp.zeros(state_ref.shape, F32)

    lane = lax.broadcasted_iota(jnp.int32, (c, PAIR), 1)
    left = lane < HEAD_DIM
    same = same_ref[...]
    head_mean = (same * (1.0 / HEAD_DIM)).astype(BF16)

    units = [(p, lo) for p in range(n_pairs) for lo in range(0, q_ref.shape[1], c)]
    blk = lambda ref, p, lo: ref[0, lo:lo + c, p * PAIR:(p + 1) * PAIR]
    zero = jnp.zeros((c, PAIR), BF16)
    scores = {}
    for p, lo in units:
        qp, kp = blk(q_ref, p, lo), blk(k_ref, p, lo)
        scores[p, lo] = (_dot_nt(jnp.where(left, qp, zero), kp),
                         _dot_nt(jnp.where(left, zero, qp), kp))
    kv = {}
    for p, lo in units:
        k_dec = (blk(k_ref, p, lo).astype(F32) * zeta_ref[:, p * PAIR:(p + 1) * PAIR]).astype(BF16)
        kv[p, lo] = _dot_tn(k_dec, blk(v_ref, p, lo)) * same
    intra = {}
    for p, lo in units:
        s0, s1 = scores[p, lo]
        vp = blk(v_ref, p, lo)
        intra[p, lo] = jnp.where(left,
                                 _dot((s0 * intra_ref[2 * p]).astype(BF16), vp),
                                 _dot((s1 * intra_ref[2 * p + 1]).astype(BF16), vp))
    outs = {}
    for p in range(n_pairs):
        state = state_ref[p]
        for lo in range(0, q_ref.shape[1], c):
            q_dec = (blk(q_ref, p, lo).astype(F32) * xi_ref[:, p * PAIR:(p + 1) * PAIR]).astype(BF16)
            outs[p, lo] = intra[p, lo] + _dot(q_dec, state.astype(BF16))
            state = sdec_ref[p] * state + kv[p, lo]
        state_ref[p] = state
    mus = {u: _dot(outs[u].astype(BF16), head_mean) for u in units}
    xcs = {u: outs[u] - mus[u] for u in units}
    variances = {u: _dot((xcs[u] * xcs[u]).astype(BF16), head_mean) for u in units}
    for p, lo in units:
        y = xcs[p, lo] * lax.rsqrt(variances[p, lo] + NORM_EPS)
        g = blk(g_ref, p, lo).astype(F32)
        o_ref[0, lo:lo + c, p * PAIR:(p + 1) * PAIR] = (g * _sigmoid(g) * y).astype(BF16)


def _retention(proj3d, width):
    batch, seq, _ = proj3d.shape
    n_heads = width // HEAD_DIM
    tc = min(RET_STEP, seq)
    intra, zeta, xi, sdec, same = _retention_tables(n_heads)
    col_spec = lambda i: pl.BlockSpec((1, tc, width), lambda b, j: (b, j, i))
    return pl.pallas_call(
        _retention_kernel,
        grid=(batch, seq // tc),
        in_specs=[col_spec(0), col_spec(1), col_spec(2), col_spec(3),
                  _resident(intra.shape), _resident(zeta.shape), _resident(xi.shape),
                  _resident(sdec.shape), _resident(same.shape)],
        out_specs=pl.BlockSpec((1, tc, width), lambda b, j: (b, j, 0)),
        out_shape=jax.ShapeDtypeStruct((batch, seq, width), BF16),
        scratch_shapes=[pltpu.VMEM((n_heads // 2, PAIR, PAIR), F32)],
        compiler_params=_params("parallel", "arbitrary"),
        name="retention",
    )(proj3d, proj3d, proj3d, proj3d, intra, zeta, xi, sdec, same)


def _s5_tables(lam_re, lam_im, b_re, b_im, c_re, c_im, log_step):
    n_groups, n_state = lam_re.shape
    n_chan = b_re.shape[-1]
    L = S5_CHUNK
    lr, li = lam_re.astype(F32), lam_im.astype(F32)
    delta = jnp.exp(log_step.astype(F32))[:, None]
    steps = jnp.arange(L + 1, dtype=F32)[:, None, None]
    mag = jnp.exp(steps * (lr * delta))
    pow_re = mag * jnp.cos(steps * (li * delta))
    pow_im = mag * jnp.sin(steps * (li * delta))
    bar_re, bar_im = pow_re[1], pow_im[1]
    den = lr * lr + li * li
    coef_re = ((bar_re - 1.0) * lr + bar_im * li) / den
    coef_im = (bar_im * lr - (bar_re - 1.0) * li) / den
    br, bi = b_re.astype(F32), b_im.astype(F32)
    bbar_re = coef_re[:, :, None] * br - coef_im[:, :, None] * bi
    bbar_im = coef_re[:, :, None] * bi + coef_im[:, :, None] * br
    gpk = lambda t: jnp.transpose(t, (1, 2, 0))
    crt, cit = jnp.swapaxes(c_re.astype(F32), 1, 2), jnp.swapaxes(c_im.astype(F32), 1, 2)
    pr, pi = gpk(pow_re)[..., None], gpk(pow_im)[..., None]
    cp_re = crt[:, :, None, :] * pr - cit[:, :, None, :] * pi
    cp_im = crt[:, :, None, :] * pi + cit[:, :, None, :] * pr
    kern = (jnp.einsum('gpkc,gpd->gdkc', cp_re, bbar_re)
            - jnp.einsum('gpkc,gpd->gdkc', cp_im, bbar_im))
    flat = kern[:, :, :L].reshape(n_groups, n_chan, L * n_chan)
    toe = jnp.concatenate([jnp.pad(flat, ((0, 0), (0, 0), (s * n_chan, 0)))[:, :, :L * n_chan]
                           for s in range(L)], axis=1)
    quad = lambda a, b, c, d: jnp.concatenate([a, b, c, d], axis=-1)
    dec = lambda t: jnp.transpose(t[L - 1 - jnp.arange(L)], (1, 0, 2))[:, :, None, :]
    dr, di = dec(pow_re), dec(pow_im)
    brt, bit = jnp.swapaxes(bbar_re, 1, 2)[:, None], jnp.swapaxes(bbar_im, 1, 2)[:, None]
    b_pow = (quad(dr, dr, dr, dr) * quad(brt, bit, bit, brt)
             + quad(di, di, di, di) * quad(-bit, brt, brt, -bit))
    b_pow = b_pow.reshape(n_groups, L * n_chan, 4 * n_state)
    from_state = lambda m: m[:, :, 1:].reshape(n_groups, n_state, L * n_chan)
    c_pow = jnp.concatenate([from_state(cp_re), -from_state(cp_im)], axis=1)
    a1 = jnp.concatenate([pow_re[L], pow_re[L]], axis=-1).reshape(1, n_groups * 2 * n_state)
    a2 = jnp.concatenate([-pow_im[L], pow_im[L]], axis=-1).reshape(1, n_groups * 2 * n_state)
    return toe.astype(BF16), b_pow.astype(BF16), c_pow.astype(BF16), a1, a2


def _granule_transpose(x):
    n = len(x)
    half = n // 2
    lane = lax.broadcasted_iota(jnp.int32, (1, LANES), 1)

    def rotate(cols, g):
        c0, c1 = cols
        if g >= half:
            c0, c1, g = c1, c0, g - half
        if g == 0:
            return [c0, c1]
        s = g * S5_GROUP
        r0, r1 = pltpu.roll(c0, s, 1), pltpu.roll(c1, s, 1)
        wrapped = lane < s
        return [jnp.where(wrapped, r1, r0), jnp.where(wrapped, r0, r1)]

    w = [rotate(x[i], i) for i in range(n)]
    for bit in range(3):
        d = 1 << bit
        take = ((lane // S5_GROUP) >> bit) & 1 == 1
        w = [[jnp.where(take, w[(j + d) % n][col], w[j][col]) for col in range(2)]
             for j in range(n)]
    w = [[w[j][0], w[(j + half) % n][1]] for j in range(n)]
    return [rotate(w[(-j) % n], (-j) % n) for j in range(n)]


def _s5_kernel(u_ref, toe_ref, bpow_ref, cpow_ref, a1_ref, a2_ref, d_ref, wglu_ref, o_ref,
               carry_ref, *, batch):
    n_chunks = u_ref.shape[0]
    rows = n_chunks * batch
    width = u_ref.shape[2]
    st = 2 * S5_STATE
    fold = S5_CHUNK * S5_GROUP

    @pl.when(pl.program_id(0) == 0)
    def _():
        carry_ref[...] = jnp.zeros(carry_ref.shape, F32)

    def slab(t):
        return u_ref[:, t * batch:(t + 1) * batch, :].reshape(rows, width)

    n_half = fold // S5_GROUP
    y_cols = [[None] * (width // LANES) for _ in range(S5_CHUNK)]
    for half in range(width // fold):
        base = half * fold
        x = _granule_transpose(
            [[slab(t)[:, base + c * LANES: base + (c + 1) * LANES] for c in range(2)]
             for t in range(S5_CHUNK)])
        y_half = []
        for j0 in range(0, n_half, S5_GROUP_BATCH):
            groups = range(half * n_half + j0, half * n_half + j0 + S5_GROUP_BATCH)
            folded = {g: jnp.concatenate(x[g % n_half], axis=1).astype(BF16) for g in groups}
            x_end = {g: _dot(folded[g], bpow_ref[g]) for g in groups}
            x_start = {}
            for g in groups:
                lanes = slice(g * st, (g + 1) * st)
                a1, a2 = a1_ref[:, lanes], a2_ref[:, lanes]
                xs, xw = carry_ref[0, :, lanes], carry_ref[1, :, lanes]
                starts = []
                for ch in range(n_chunks):
                    starts.append(xs)
                    e = x_end[g][ch * batch:(ch + 1) * batch]
                    xs, xw = a1 * xs + a2 * xw + e[:, :st], a1 * xw - a2 * xs + e[:, st:]
                carry_ref[0, :, lanes] = xs
                carry_ref[1, :, lanes] = xw
                x_start[g] = jnp.concatenate(starts, axis=0).astype(BF16)
            for g in groups:
                y = _dot(folded[g], toe_ref[g]) + _dot(x_start[g], cpow_ref[g])
                y_half.append([y[:, :LANES], y[:, LANES:]])
        y_t = _granule_transpose(y_half)
        for t in range(S5_CHUNK):
            y_cols[t][2 * half], y_cols[t][2 * half + 1] = y_t[t]

    for t in range(0, S5_CHUNK, 2):
        outs = []
        for tt in (t, t + 1):
            y = jnp.concatenate(y_cols[tt], axis=1) + d_ref[...] * slab(tt)
            y = jax.nn.gelu(y)
            z = y * _sigmoid(_dot(y.astype(BF16), wglu_ref[...]))
            outs.append(z.reshape(n_chunks, batch, width))
        o_ref[:, t * batch:(t + 2) * batch, :] = jnp.concatenate(outs, axis=1).astype(BF16)


def _s5(u_tm, batch, toe, b_pow, c_pow, a1, a2, d_skip, w_glu):
    seq, _, width = u_tm.shape
    span = S5_CHUNK * batch
    n_chunks = min(S5_STEP_CHUNKS, seq // S5_CHUNK)
    total = seq // S5_CHUNK
    u3 = u_tm.reshape(total, span, width)
    out = pl.pallas_call(
        functools.partial(_s5_kernel, batch=batch),
        grid=(total // n_chunks,),
        in_specs=[pl.BlockSpec((n_chunks, span, width), lambda j: (j, 0, 0)),
                  _resident(toe.shape), _resident(b_pow.shape), _resident(c_pow.shape),
                  _resident(a1.shape), _resident(a2.shape),
                  _resident((1, width)), _resident(w_glu.shape)],
        out_specs=pl.BlockSpec((n_chunks, span, width), lambda j: (j, 0, 0)),
        out_shape=jax.ShapeDtypeStruct((total, span, width), BF16),
        scratch_shapes=[pltpu.VMEM((2, batch, a1.shape[1]), F32)],
        compiler_params=_params("arbitrary"),
        name="s5_chunked",
    )(u3, toe, b_pow, c_pow, a1, a2, d_skip.reshape(1, width).astype(F32), w_glu.astype(BF16))
    return out.reshape(seq * batch, width)


def _diff_attn_kernel(q_ref, k_ref, v_ref, lam_ref, subln_ref, o_ref,
                      vt_ref, qm_ref, sa_ref, sb_ref, m_ref, acc_ref, *, tile):
    tq, tk = tile
    ratio = tq // tk
    seq = k_ref.shape[1]
    n_tiles = seq // tq
    lane = lax.broadcasted_iota(jnp.int32, (tk, PAIR), 1)
    zero = jnp.zeros((tk, PAIR), BF16)
    for lo in range(0, seq, tk):
        vt_ref[0:PAIR, lo:lo + tk] = v_ref[0, lo:lo + tk, :].astype(F32).T.astype(BF16)
        q = q_ref[0, lo:lo + tk, :]
        qm_ref[0, lo:lo + tk, :] = jnp.where(lane < HEAD_DIM, q, zero)
        qm_ref[1, lo:lo + tk, :] = jnp.where(lane < HEAD_DIM, zero, q)
    vt_ref[PAIR:, :] = jnp.ones((vt_ref.shape[0] - PAIR, seq), BF16)

    lam_p = lam_ref[...]
    lam = (jnp.exp(jnp.sum(lam_p[0:1] * lam_p[1:2], axis=-1, keepdims=True))
           - jnp.exp(jnp.sum(lam_p[2:3] * lam_p[3:4], axis=-1, keepdims=True))
           + LAMBDA_INIT_ODD)
    out_gain = subln_ref[...] * (1.0 - LAMBDA_INIT_ODD)

    def scores(qi, kj, i, s_ref, d=0):
        k = k_ref[0, pl.ds(pl.multiple_of(kj * tk, tk), tk), :]
        q = qm_ref[i, pl.ds(pl.multiple_of(qi * tq + d * tk, tk), tq - d * tk), :]
        s_ref[:, d * tk:] = _dot_nt(k, q).astype(BF16)

    def consume(kj, i, s_ref, d=None):
        cols = slice(0, tq) if d is None else slice(d * tk, tq)
        s = s_ref[:, cols]
        if d is not None:
            k_pos = lax.broadcasted_iota(jnp.int32, s.shape, 0)
            q_pos = lax.broadcasted_iota(jnp.int32, s.shape, 1)
            s = jnp.where(k_pos <= q_pos, s, jnp.asarray(-jnp.inf, BF16))
        m_prev = m_ref[i, :, cols]
        m_part = jnp.max(s.reshape(tk // 16, 16, s.shape[1]), axis=0)
        m_new = jnp.maximum(m_prev, jnp.max(m_part, axis=0, keepdims=True).astype(F32))
        alpha = jnp.exp2(m_prev - m_new)
        p = jnp.exp2(s - m_new.astype(BF16))
        v_t = vt_ref[:, pl.ds(pl.multiple_of(kj * tk, tk), tk)]
        acc_ref[i, :, cols] = alpha * acc_ref[i, :, cols] + _dot(v_t, p)
        m_ref[i, :, cols] = m_new

    scores(0, 0, 0, sa_ref)

    def tile_body(qi, carry):
        m_ref[...] = jnp.full(m_ref.shape, -jnp.inf, F32)
        acc_ref[...] = jnp.zeros(acc_ref.shape, F32)
        first = qi * ratio

        def full_body(kj, c):
            scores(qi, kj, 1, sb_ref)
            consume(kj, 0, sa_ref)
            scores(qi, kj + 1, 0, sa_ref)
            consume(kj, 1, sb_ref)
            return c

        lax.fori_loop(0, first, full_body, 0)
        for d in range(ratio):
            scores(qi, first + d, 1, sb_ref, d)
            consume(first + d, 0, sa_ref, d)
            if d + 1 < ratio:
                scores(qi, first + d + 1, 0, sa_ref, d + 1)
            else:
                scores(jnp.minimum(qi + 1, n_tiles - 1), 0, 0, sa_ref)
            consume(first + d, 1, sb_ref, d)

        acc0, acc1 = acc_ref[0], acc_ref[1]
        out = (acc0[:PAIR] * (1.0 / acc0[PAIR:PAIR + 1])
               - lam * (acc1[:PAIR] * (1.0 / acc1[PAIR:PAIR + 1])))
        ms = jnp.mean(out * out, axis=0, keepdims=True)
        out = out * lax.rsqrt(ms + NORM_EPS) * out_gain
        o_ref[0, pl.ds(pl.multiple_of(qi * tq, tq), tq), :] = out.T.astype(BF16)
        return carry

    lax.fori_loop(0, n_tiles, tile_body, 0)


def _diff_attention(qkv3d, n_heads, lam_params, subln):
    batch, seq, _ = qkv3d.shape
    tq, tk = min(ATTN_Q_TILE, seq), min(ATTN_K_BLOCK, seq)
    ones_rows = 16
    seq_spec = lambda off: pl.BlockSpec((1, seq, PAIR), lambda b, h: (b, 0, off + h))
    return pl.pallas_call(
        functools.partial(_diff_attn_kernel, tile=(tq, tk)),
        grid=(batch, n_heads),
        in_specs=[seq_spec(0), seq_spec(n_heads), seq_spec(2 * n_heads),
                  _resident(lam_params.shape), _resident((PAIR, 1))],
        out_specs=seq_spec(0),
        out_shape=jax.ShapeDtypeStruct((batch, seq, n_heads * PAIR), BF16),
        scratch_shapes=[pltpu.VMEM((PAIR + ones_rows, seq), BF16),
                        pltpu.VMEM((2, seq, PAIR), BF16),
                        pltpu.VMEM((tk, tq), BF16), pltpu.VMEM((tk, tq), BF16),
                        pltpu.VMEM((2, 1, tq), F32),
                        pltpu.VMEM((2, PAIR + ones_rows, tq), F32)],
        compiler_params=_params("parallel", "parallel"),
        name="diff_attention",
    )(qkv3d, qkv3d, qkv3d, lam_params, subln.reshape(PAIR, 1).astype(F32))


def _ffn_chunks(hidden):
    bounds = list(range(0, hidden, 1024)) + [hidden]
    return list(zip(bounds[:-1], bounds[1:]))


def _post_kernel(*refs, time_major, final):
    n_mix = len(time_major)
    h_ref = refs[0]
    mix_refs = refs[1:1 + n_mix]
    rest = list(refs[1 + n_mix:])
    perm_ref = rest.pop(1) if any(time_major) else None
    (p_ref, wo_ref, nf_ref, wg_ref, wu_ref, wd_ref, np_ref, wpg_ref, wpp_ref,
     fin_ref, o_ref) = rest
    rows = h_ref.shape[0] * h_ref.shape[1]
    flat = lambda ref: ref[...].reshape(rows, ref.shape[-1])
    h = flat(h_ref)
    row = 0
    for m_ref, is_time_major in zip(mix_refs, time_major):
        mix = flat(m_ref)
        if is_time_major:
            mix = _dot(perm_ref[...], mix).astype(BF16)
        width = mix.shape[1]
        h = h + _dot(mix, wo_ref[row:row + width, :])
        row += width
    hn = _rms(h, nf_ref[...]).astype(BF16)
    ffn = jnp.zeros_like(h)
    for lo, hi in _ffn_chunks(wg_ref.shape[1]):
        gate = _dot(hn, wg_ref[:, lo:hi])
        up = _dot(hn, wu_ref[:, lo:hi])
        ffn = ffn + _dot((gate * _sigmoid(gate) * up).astype(BF16), wd_ref[lo:hi, :])
    h = h + ffn
    hp = _rms(h, np_ref[...]).astype(BF16)
    ple_gate = _sigmoid(_dot(hp, wpg_ref[...]))
    h = h + _dot(flat(p_ref).astype(BF16), wpp_ref[...]) * ple_gate
    if final:
        h = _rms(h, fin_ref[...])
    o_ref[...] = h.reshape(o_ref.shape)


def _post_mixer(h3d, mixes, p4d, layer, w_out, norm_ffn, w_gate, w_up, w_down,
                norm_ple, w_ple_gate, w_ple_proj, final_norm, *, final):
    batch, seq, d_model = h3d.shape
    tt = min(ROW_TILE // batch, seq)
    rows = tt * batch
    bm_spec = lambda width: pl.BlockSpec((batch, tt, width), lambda j: (0, j, 0))
    tm_spec = lambda width: pl.BlockSpec((1, rows, width), lambda j: (j, 0, 0))
    p_spec = pl.BlockSpec((None, batch, tt, p4d.shape[3]), lambda j: (layer, 0, j, 0))
    vec = lambda v: v.reshape(v.shape[0], 1, d_model).astype(F32)
    weights = [w_out.astype(BF16), vec(norm_ffn), w_gate.astype(BF16), w_up.astype(BF16),
               w_down.astype(BF16), vec(norm_ple), w_ple_gate.astype(BF16),
               w_ple_proj.astype(BF16), vec(final_norm[None])]
    layer_of = [0, layer, layer, layer, layer, layer, layer, layer, 0]
    weight_specs = [pl.BlockSpec((None,) + w.shape[1:], lambda j, i=i: (i, 0, 0),
                                 pipeline_mode=pl.Buffered(1))
                    for w, i in zip(weights, layer_of)]
    time_major = tuple(tm for _, tm in mixes)
    mix_arrays = [m.reshape(seq // tt, rows, m.shape[-1]) if tm else m for m, tm in mixes]
    mix_specs = [(tm_spec if tm else bm_spec)(m.shape[-1]) for m, tm in mixes]
    extra, extra_specs = [], []
    if any(time_major):
        dst = jnp.arange(rows)
        src = (dst % tt) * batch + dst // tt
        extra = [(src[:, None] == jnp.arange(rows)[None, :]).astype(BF16)]
        extra_specs = [_resident((rows, rows))]
    return pl.pallas_call(
        functools.partial(_post_kernel, time_major=time_major, final=final),
        grid=(seq // tt,),
        in_specs=([bm_spec(d_model)] + mix_specs + [p_spec] + extra_specs
                  + weight_specs),
        out_specs=bm_spec(d_model),
        out_shape=jax.ShapeDtypeStruct((batch, seq, d_model), F32),
        compiler_params=_params("parallel"),
        name="outproj_ffn_ple",
    )(h3d, *mix_arrays, p4d, *extra, *weights)


def kernel(x, p, norm_mix, norm_ffn, norm_ple, ret_s5_w_in, ret_s5_w_out, s5_lambda_re, s5_lambda_im, s5_b_re, s5_b_im, s5_c_re, s5_c_im, s5_d, s5_log_step, s5_w_glu, diff_w_qkv, diff_w_o, diff_lambda_q1, diff_lambda_k1, diff_lambda_q2, diff_lambda_k2, diff_subln, ffn_w_gate, ffn_w_up, ffn_w_down, ple_w_proj, ple_w_gate, final_norm):
    batch, seq, d_model = x.shape
    tokens = batch * seq
    p4d = p.reshape(p.shape[0], batch, seq, p.shape[-1])

    s5_width = s5_w_glu.shape[-1]
    ret_width = ret_s5_w_in.shape[-1] - s5_width
    ret_width //= 4
    proj, u_tm = _norm_proj(x.reshape(tokens, d_model), norm_mix[0], ret_s5_w_in[0], batch, seq,
                            n_q=ret_width, n_k=ret_width, n_main=4 * ret_width,
                            q_scale=HEAD_DIM ** -0.5)
    ret = _retention(proj.reshape(batch, seq, 4 * ret_width), ret_width)
    toe, b_pow, c_pow, a1, a2 = _s5_tables(s5_lambda_re[0], s5_lambda_im[0], s5_b_re[0], s5_b_im[0],
                                           s5_c_re[0], s5_c_im[0], s5_log_step[0])
    ssm_tm = _s5(u_tm, batch, toe, b_pow, c_pow, a1, a2, s5_d[0], s5_w_glu[0])
    h = _post_mixer(x, [(ret, False), (ssm_tm, True)],
                    p4d, 0, ret_s5_w_out, norm_ffn, ffn_w_gate, ffn_w_up,
                    ffn_w_down, norm_ple, ple_w_gate, ple_w_proj, final_norm,
                    final=False)

    v_width = diff_w_o.shape[1]
    n_heads = v_width // PAIR
    (qkv,) = _norm_proj(h.reshape(tokens, d_model), norm_mix[1], diff_w_qkv[0], batch, seq,
                        n_q=v_width, n_k=v_width, n_main=3 * v_width,
                        q_scale=HEAD_DIM ** -0.5 * math.log2(math.e))
    lam_params = jnp.stack([diff_lambda_q1[0], diff_lambda_k1[0],
                            diff_lambda_q2[0], diff_lambda_k2[0]]).astype(F32)
    attn = _diff_attention(qkv.reshape(batch, seq, 3 * v_width), n_heads, lam_params, diff_subln[0])
    return _post_mixer(h, [(attn, False)],
                       p4d, 1, diff_w_o, norm_ffn, ffn_w_gate, ffn_w_up,
                       ffn_w_down, norm_ple, ple_w_gate, ple_w_proj, final_norm,
                       final=True)
```
